```python
import math
import jax
import jax.numpy as jnp
from jax import lax
import numpy as np

D_MODEL = 4096
BATCH = 1
SEQ = 16384
DEPTH = 2

N_MIXERS = 4
GROUP_W = D_MODEL // N_MIXERS
CONV_W = 3
RWKV_HEAD = 64
RWKV_HEADS = GROUP_W // RWKV_HEAD
DECAY_LORA = max(32, int(round(1.8 * GROUP_W ** 0.5 / 32)) * 32)
AAA_LORA = max(32, int(round(1.8 * GROUP_W ** 0.5 / 32)) * 32)
MV_LORA = max(32, int(round(1.3 * GROUP_W ** 0.5 / 32)) * 32)
GATE_LORA = max(32, int(round(0.6 * GROUP_W ** 0.8 / 32)) * 32)
RWKV_GN_EPS = 64e-5
SGU_CHUNK = 128
SGU_HEAD = 128
SGU_HEADS = GROUP_W // SGU_HEAD
POOL_WINDOWS = (2, 4, 8, 16)
POOL_GROUP = GROUP_W // len(POOL_WINDOWS)
FFN_HIDDEN = ((8 * D_MODEL + 2) // 3 + 255) // 256 * 256
DEEPNORM_ALPHA = (2 * DEPTH) ** 0.25
DEEPNORM_BETA = (8 * DEPTH) ** -0.25
LN_EPS = 1e-5

kernel_name = "hybrid_parallel_groups_conv_rwkv7_sgu_pool_deepnorm"


def _rwkv_cols(layer):
    return 3 * GROUP_W + DECAY_LORA + AAA_LORA + GATE_LORA + (MV_LORA if layer > 0 else 0)


def _in_sizes(layer):
    return (GROUP_W, GROUP_W, GROUP_W, _rwkv_cols(layer), GROUP_W, GROUP_W, GROUP_W)


def _split(z, sizes):
    idx = []
    s = 0
    for n in sizes[:-1]:
        s += n
        idx.append(s)
    return jnp.split(z, idx, axis=-1)


def _shift(z):
    return jnp.pad(z, ((0, 0), (1, 0), (0, 0)))[:, :-1]


def _layer_norm(x, g, b, eps=LN_EPS):
    xf = x.astype(jnp.float32)
    mu = jnp.mean(xf, axis=-1, keepdims=True)
    var = jnp.mean(jnp.square(xf - mu), axis=-1, keepdims=True)
    return ((xf - mu) * lax.rsqrt(var + eps) * g + b).astype(x.dtype)


def _short_conv_mixer(h, bg, cg, conv_w):
    T = h.shape[1]
    z = cg * h
    zp = jnp.pad(z, ((0, 0), (CONV_W - 1, 0), (0, 0)))
    conv = zp[:, 0:T] * conv_w[0]
    for j in range(1, CONV_W):
        conv = conv + zp[:, j:j + T] * conv_w[j]
    return bg * conv


def _wkv7_scan(r, w, k, v, a, b):
    Bsz, T, H, N = r.shape

    def step(S, inp):
        r_t, w_t, k_t, v_t, a_t, b_t = inp
        sa = jnp.einsum('bhvk,bhk->bhv', S, a_t)
        S = S * w_t[:, :, None, :] + sa[..., None] * b_t[:, :, None, :] + v_t[..., None] * k_t[:, :, None, :]
        y = jnp.einsum('bhvk,bhk->bhv', S, r_t)
        return S, y

    xs = tuple(jnp.moveaxis(z, 1, 0) for z in (r, w, k, v, a, b))
    S0 = jnp.zeros((Bsz, H, N, N), jnp.float32)
    _, ys = lax.scan(step, S0, xs)
    return jnp.moveaxis(ys, 0, 1)


def _rwkv7_mixer(cols, v_first, layer, p):
    Bsz, T, _ = cols.shape
    dt = cols.dtype
    c = cols.astype(jnp.float32)
    c = c + (_shift(c) - c) * p['shift_mu']
    sizes = (GROUP_W, GROUP_W, GROUP_W, DECAY_LORA, AAA_LORA, GATE_LORA) + ((MV_LORA,) if layer > 0 else ())
    parts = _split(c, sizes)
    r, k, v, wd, ad, gd = parts[:6]
    w = -jax.nn.softplus(-(p['decay_w0'] + jnp.einsum('btr,rc->btc', jnp.tanh(wd), p['decay_up']))) - 0.5
    decay = jnp.exp(-jnp.exp(w))
    a = jax.nn.sigmoid(p['iclr_a0'] + jnp.einsum('btr,rc->btc', ad, p['iclr_up']))
    g = jnp.einsum('btr,rc->btc', jax.nn.sigmoid(gd), p['gate_up'])
    if layer == 0:
        v_first = v
    else:
        v = v + (v_first - v) * jax.nn.sigmoid(p['vres_v0'] + jnp.einsum('btr,rc->btc', parts[6], p['vres_up']))

    def heads(z):
        return z.reshape(Bsz, T, RWKV_HEADS, RWKV_HEAD)

    kk = heads(k * p['k_k'])
    kk = kk / jnp.maximum(jnp.sqrt(jnp.sum(jnp.square(kk), axis=-1, keepdims=True)), 1e-12)
    k = k * (1.0 + (a - 1.0) * p['k_a'])
    r_h, k_h, v_h, a_h = heads(r), heads(k), heads(v), heads(a)
    y = _wkv7_scan(r_h, heads(decay), k_h, v_h, -kk, kk * a_h)
    mu = jnp.mean(y, axis=-1, keepdims=True)
    var = jnp.mean(jnp.square(y - mu), axis=-1, keepdims=True)
    y = ((y - mu) * lax.rsqrt(var + RWKV_GN_EPS)).reshape(Bsz, T, GROUP_W) * p['lnx_g'] + p['lnx_b']
    y = y + (jnp.sum(r_h * k_h * p['r_k'], axis=-1, keepdims=True) * v_h).reshape(Bsz, T, GROUP_W)
    return (y * g).astype(dt), v_first


def _spatial_gating_mixer(u, v, ln_g, ln_b, sgu_w, sgu_b):
    Bsz, T, _ = v.shape
    u = jax.nn.gelu(u)
    v = _layer_norm(jax.nn.gelu(v), ln_g, ln_b)
    vc = v.reshape(Bsz, T // SGU_CHUNK, SGU_CHUNK, SGU_HEADS, SGU_HEAD)
    causal = jnp.tril(jnp.ones((SGU_CHUNK, SGU_CHUNK), dtype=bool))
    w = jnp.where(causal[None], sgu_w, jnp.zeros((), sgu_w.dtype))
    s = jnp.einsum('hts,bcshd->bcthd', w, vc) + sgu_b.T[None, None, :, :, None]
    return u * s.reshape(Bsz, T, GROUP_W)


def _pool_mixer(z, pool_w, pool_scale):
    Bsz, T, _ = z.shape
    zf = z.astype(jnp.float32)
    cs = jnp.cumsum(zf, axis=1)
    pos = jnp.arange(T)
    outs = []
    for i, win in enumerate(POOL_WINDOWS):
        cs_i = cs[..., i * POOL_GROUP:(i + 1) * POOL_GROUP]
        prev = jnp.pad(cs_i, ((0, 0), (win, 0), (0, 0)))[:, :T]
        cnt = jnp.minimum(pos + 1, win).astype(jnp.float32)[None, :, None]
        outs.append((cs_i - prev) / cnt - zf[..., i * POOL_GROUP:(i + 1) * POOL_GROUP])
    d = jnp.stack(outs, axis=2)
    y = jnp.einsum('btgc,gce->btge', d, pool_w).reshape(Bsz, T, GROUP_W)
    return (y * pool_scale).astype(z.dtype)


def _hybrid_layer(x, v_first, layer, p):
    proj = jnp.einsum('btd,dc->btc', x, p['w_in'])
    h, bg, cg, rw, su, sv, pz = _split(proj, _in_sizes(layer))
    y_conv = _short_conv_mixer(h, bg, cg, p['conv_w'])
    y_rwkv, v_first = _rwkv7_mixer(rw, v_first, layer, p)
    y_sgu = _spatial_gating_mixer(su, sv, p['sgu_ln_g'], p['sgu_ln_b'], p['sgu_w'], p['sgu_b'])
    y_pool = _pool_mixer(pz, p['pool_w'], p['pool_scale'])
    mix = jnp.concatenate([y_conv, y_rwkv, y_sgu, y_pool], axis=-1)
    mix_out = jnp.einsum('btc,cd->btd', mix, p['w_out'])
    x = _layer_norm(DEEPNORM_ALPHA * x + mix_out, p['ln_mix_g'], p['ln_mix_b'])
    hid = jax.nn.silu(jnp.einsum('btd,df->btf', x, p['ffn_gate'])) * jnp.einsum('btd,df->btf', x, p['ffn_up'])
    ffn_out = jnp.einsum('btf,fd->btd', hid, p['ffn_down'])
    x = _layer_norm(DEEPNORM_ALPHA * x + ffn_out, p['ln_ffn_g'], p['ln_ffn_b'])
    return x, v_first


def _layer_params(key, layer):
    ks = iter(jax.random.split(key, 40))
    f32 = jnp.float32

    def nrm(shape, scale):
        return scale * jax.random.normal(next(ks), shape, f32)

    p = {}
    p['w_in'] = nrm((D_MODEL, sum(_in_sizes(layer))), D_MODEL ** -0.5)
    p['conv_w'] = nrm((CONV_W, GROUP_W), CONV_W ** -0.5)
    p['shift_mu'] = jax.random.uniform(next(ks), (_rwkv_cols(layer),), f32)
    p['decay_w0'] = jax.random.uniform(next(ks), (GROUP_W,), f32, -6.0, -1.0)
    p['decay_up'] = nrm((DECAY_LORA, GROUP_W), 0.1 * DECAY_LORA ** -0.5)
    p['iclr_a0'] = nrm((GROUP_W,), 0.1)
    p['iclr_up'] = nrm((AAA_LORA, GROUP_W), AAA_LORA ** -0.5)
    if layer > 0:
        p['vres_v0'] = nrm((GROUP_W,), 0.1)
        p['vres_up'] = nrm((MV_LORA, GROUP_W), MV_LORA ** -0.5)
    p['gate_up'] = nrm((GATE_LORA, GROUP_W), GATE_LORA ** -0.5)
    p['k_k'] = 0.85 + nrm((GROUP_W,), 0.02)
    p['k_a'] = 1.0 + nrm((GROUP_W,), 0.02)
    p['r_k'] = nrm((RWKV_HEADS, RWKV_HEAD), 0.1)
    p['lnx_g'] = 1.0 + nrm((GROUP_W,), 0.02)
    p['lnx_b'] = nrm((GROUP_W,), 0.02)
    p['sgu_ln_g'] = 1.0 + nrm((GROUP_W,), 0.02)
    p['sgu_ln_b'] = nrm((GROUP_W,), 0.02)
    p['sgu_w'] = nrm((SGU_HEADS, SGU_CHUNK, SGU_CHUNK), SGU_CHUNK ** -0.5)
    p['sgu_b'] = 1.0 + nrm((SGU_HEADS, SGU_CHUNK), 0.02)
    p['pool_w'] = nrm((len(POOL_WINDOWS), POOL_GROUP, POOL_GROUP), POOL_GROUP ** -0.5)
    p['pool_scale'] = 0.5 + nrm((GROUP_W,), 0.05)
    p['w_out'] = nrm((D_MODEL, D_MODEL), DEEPNORM_BETA * D_MODEL ** -0.5)
    p['ln_mix_g'] = 1.0 + nrm((D_MODEL,), 0.02)
    p['ln_mix_b'] = nrm((D_MODEL,), 0.02)
    p['ffn_gate'] = nrm((D_MODEL, FFN_HIDDEN), D_MODEL ** -0.5)
    p['ffn_up'] = nrm((D_MODEL, FFN_HIDDEN), D_MODEL ** -0.5)
    p['ffn_down'] = nrm((FFN_HIDDEN, D_MODEL), DEEPNORM_BETA * FFN_HIDDEN ** -0.5)
    p['ln_ffn_g'] = 1.0 + nrm((D_MODEL,), 0.02)
    p['ln_ffn_b'] = nrm((D_MODEL,), 0.02)
    return p


def setup_inputs(seed: int = 0) -> dict:
    key = jax.random.key(seed)
    keys = jax.random.split(key, DEPTH + 1)
    inputs = {'x': jax.random.normal(keys[0], (BATCH, SEQ, D_MODEL), jnp.float32)}
    for layer in range(DEPTH):
        for name, val in _layer_params(keys[layer + 1], layer).items():
            inputs[name + '_' + str(layer)] = val
    return inputs


def reference(x,
              w_in_0, conv_w_0, shift_mu_0, decay_w0_0, decay_up_0, iclr_a0_0, iclr_up_0, gate_up_0,
              k_k_0, k_a_0, r_k_0, lnx_g_0, lnx_b_0, sgu_ln_g_0, sgu_ln_b_0, sgu_w_0, sgu_b_0,
              pool_w_0, pool_scale_0, w_out_0, ln_mix_g_0, ln_mix_b_0, ffn_gate_0, ffn_up_0, ffn_down_0,
              ln_ffn_g_0, ln_ffn_b_0,
              w_in_1, conv_w_1, shift_mu_1, decay_w0_1, decay_up_1, iclr_a0_1, iclr_up_1, vres_v0_1, vres_up_1,
              gate_up_1, k_k_1, k_a_1, r_k_1, lnx_g_1, lnx_b_1, sgu_ln_g_1, sgu_ln_b_1, sgu_w_1, sgu_b_1,
              pool_w_1, pool_scale_1, w_out_1, ln_mix_g_1, ln_mix_b_1, ffn_gate_1, ffn_up_1, ffn_down_1,
              ln_ffn_g_1, ln_ffn_b_1):
    layers = [
        dict(w_in=w_in_0, conv_w=conv_w_0, shift_mu=shift_mu_0, decay_w0=decay_w0_0, decay_up=decay_up_0,
             iclr_a0=iclr_a0_0, iclr_up=iclr_up_0, gate_up=gate_up_0, k_k=k_k_0, k_a=k_a_0, r_k=r_k_0,
             lnx_g=lnx_g_0, lnx_b=lnx_b_0, sgu_ln_g=sgu_ln_g_0, sgu_ln_b=sgu_ln_b_0, sgu_w=sgu_w_0,
             sgu_b=sgu_b_0, pool_w=pool_w_0, pool_scale=pool_scale_0, w_out=w_out_0, ln_mix_g=ln_mix_g_0,
             ln_mix_b=ln_mix_b_0, ffn_gate=ffn_gate_0, ffn_up=ffn_up_0, ffn_down=ffn_down_0,
             ln_ffn_g=ln_ffn_g_0, ln_ffn_b=ln_ffn_b_0),
        dict(w_in=w_in_1, conv_w=conv_w_1, shift_mu=shift_mu_1, decay_w0=decay_w0_1, decay_up=decay_up_1,
             iclr_a0=iclr_a0_1, iclr_up=iclr_up_1, vres_v0=vres_v0_1, vres_up=vres_up_1, gate_up=gate_up_1,
             k_k=k_k_1, k_a=k_a_1, r_k=r_k_1, lnx_g=lnx_g_1, lnx_b=lnx_b_1, sgu_ln_g=sgu_ln_g_1,
             sgu_ln_b=sgu_ln_b_1, sgu_w=sgu_w_1, sgu_b=sgu_b_1, pool_w=pool_w_1, pool_scale=pool_scale_1,
             w_out=w_out_1, ln_mix_g=ln_mix_g_1, ln_mix_b=ln_mix_b_1, ffn_gate=ffn_gate_1, ffn_up=ffn_up_1,
             ffn_down=ffn_down_1, ln_ffn_g=ln_ffn_g_1, ln_ffn_b=ln_ffn_b_1),
    ]
    v_first = None
    for layer in range(DEPTH):
        x, v_first = _hybrid_layer(x, v_first, layer, layers[layer])
    return x
```

```python
import functools

import jax
import jax.numpy as jnp
from jax import lax
from jax.experimental import pallas as pl
from jax.experimental.pallas import tpu as pltpu

LANES = 128
RWKV_HEAD = 64
SGU_CHUNK = 128
SGU_HEAD = 128
CONV_W = 3
POOL_WINDOWS = (2, 4, 8, 16)
POOL_HALO = 16
CONV_HALO = 8
LN_EPS = 1e-5
RWKV_GN_EPS = 64e-5
WKV_CHUNK = 64
DEPTH = 2
DEEPNORM_ALPHA = (2 * DEPTH) ** 0.25
VMEM_LIMIT_BYTES = 60 * 1024 * 1024

F32 = jnp.float32
BF16 = jnp.bfloat16
HI = lax.Precision.HIGHEST


def _cparams(sem):
    return pltpu.CompilerParams(dimension_semantics=sem, vmem_limit_bytes=VMEM_LIMIT_BYTES)


def _pick(n, prefs):
    for p in prefs:
        if n % p == 0:
            return p
    return n


def _round_up(n, m):
    return (n + m - 1) // m * m


def _bdot(a, b):
    return jnp.dot(a.astype(BF16), b.astype(BF16), preferred_element_type=F32)


def _hdot(a, b):
    return jnp.dot(a, b, precision=HI, preferred_element_type=F32)


def _hdot_nt(a, b):
    return lax.dot_general(a, b, (((1,), (1,)), ((), ())), precision=HI, preferred_element_type=F32)


def _hdot_tn(a, b):
    return lax.dot_general(a, b, (((0,), (0,)), ((), ())), precision=HI, preferred_element_type=F32)


def _layer_norm(x, g, b, eps):
    mu = jnp.mean(x, axis=-1, keepdims=True)
    xc = x - mu
    var = jnp.mean(xc * xc, axis=-1, keepdims=True)
    return xc * lax.rsqrt(var + eps) * g + b


def _sigmoid(x):
    return 1.0 / (1.0 + jnp.exp(-x))


def _gelu_tanh(x):
    c = 0.7978845608028654
    return 0.5 * x * (1.0 + jnp.tanh(c * (x + 0.044715 * (x * x * x))))


def _softplus(x):
    return jnp.maximum(x, 0.0) + jnp.log(1.0 + jnp.exp(-jnp.abs(x)))


def _inproj_kernel(x_ref, w_ref, o_ref, xb_ref):
    @pl.when(pl.program_id(1) == 0)
    def _():
        xb_ref[...] = x_ref[...].astype(BF16)

    o_ref[...] = jnp.dot(xb_ref[...], w_ref[...], preferred_element_type=F32)


def _inproj(x, w):
    T, D = x.shape
    N = w.shape[1]
    tm = _pick(T, (512, 256, 128))
    tn = _pick(N, (512, 256, 128))
    return pl.pallas_call(
        _inproj_kernel,
        grid=(T // tm, N // tn),
        in_specs=[pl.BlockSpec((tm, D), lambda i, j: (i, 0)),
                  pl.BlockSpec((D, tn), lambda i, j: (0, j))],
        out_specs=pl.BlockSpec((tm, tn), lambda i, j: (i, j)),
        out_shape=jax.ShapeDtypeStruct((T, N), F32),
        scratch_shapes=[pltpu.VMEM((tm, D), BF16)],
        compiler_params=_cparams(("arbitrary", "arbitrary")),
        name="inproj",
    )(x, w)


def _local_mixers_kernel(h_ref, bg_ref, cg_ref, su_ref, sv_ref, pz_ref,
                         convw_ref, lng_ref, lnb_ref, sguw_ref, sgubt_ref, poolw_ref, pools_ref,
                         yconv_ref, ysgu_ref, ypool_ref,
                         zbuf_ref, pbuf_ref, *, tb):
    i = pl.program_id(0)

    @pl.when(i == 0)
    def _():
        zbuf_ref[0:CONV_HALO, :] = jnp.zeros((CONV_HALO, zbuf_ref.shape[1]), F32)
        pbuf_ref[0:POOL_HALO, :] = jnp.zeros((POOL_HALO, pbuf_ref.shape[1]), F32)

    zbuf_ref[CONV_HALO:CONV_HALO + tb, :] = cg_ref[...] * h_ref[...]
    conv = zbuf_ref[CONV_HALO - (CONV_W - 1):CONV_HALO - (CONV_W - 1) + tb, :] * convw_ref[0:1, :]
    for j in range(1, CONV_W):
        off = CONV_HALO - (CONV_W - 1 - j)
        conv = conv + zbuf_ref[off:off + tb, :] * convw_ref[j:j + 1, :]
    yconv_ref[...] = (bg_ref[...] * conv).astype(yconv_ref.dtype)
    zbuf_ref[0:CONV_HALO, :] = zbuf_ref[tb:tb + CONV_HALO, :]

    u = _gelu_tanh(su_ref[...])
    v = _layer_norm(_gelu_tanh(sv_ref[...]), lng_ref[...], lnb_ref[...], LN_EPS).astype(BF16)
    n_heads = sguw_ref.shape[0]
    row = lax.broadcasted_iota(jnp.int32, (SGU_CHUNK, SGU_CHUNK), 0)
    col = lax.broadcasted_iota(jnp.int32, (SGU_CHUNK, SGU_CHUNK), 1)
    causal = col <= row
    for hd in range(n_heads):
        w_h = jnp.where(causal, sguw_ref[hd], 0.0).astype(BF16)
        bias = sgubt_ref[:, hd:hd + 1]
        cs = slice(hd * SGU_HEAD, (hd + 1) * SGU_HEAD)
        for c in range(tb // SGU_CHUNK):
            rs = slice(c * SGU_CHUNK, (c + 1) * SGU_CHUNK)
            s = jnp.dot(w_h, v[rs, cs], preferred_element_type=F32) + bias
            ysgu_ref[rs, cs] = (u[rs, cs] * s).astype(ysgu_ref.dtype)

    pbuf_ref[POOL_HALO:POOL_HALO + tb, :] = pz_ref[...]
    pg = poolw_ref.shape[1]
    t_glob = i * tb + lax.broadcasted_iota(jnp.int32, (tb, 1), 0)
    for gi, win in enumerate(POOL_WINDOWS):
        cs = slice(gi * pg, (gi + 1) * pg)
        z = pbuf_ref[POOL_HALO:POOL_HALO + tb, cs]
        acc = z
        for j in range(1, win):
            acc = acc + pbuf_ref[POOL_HALO - j:POOL_HALO - j + tb, cs]
        cnt = jnp.minimum(t_glob + 1, win).astype(F32)
        d = acc / cnt - z
        y = _bdot(d, poolw_ref[gi])
        ypool_ref[:, cs] = (y * pools_ref[:, cs]).astype(ypool_ref.dtype)
    pbuf_ref[0:POOL_HALO, :] = pbuf_ref[tb:tb + POOL_HALO, :]


def _local_mixers(proj, G, p):
    T = proj.shape[0]
    tb = _pick(T, (256, 128))
    n_sgu = G // SGU_HEAD
    col = lambda c: pl.BlockSpec((tb, G), lambda i, c=c: (i, c))
    full = lambda a: pl.BlockSpec(a.shape, lambda i, n=a.ndim: (0,) * n)
    convw = p['conv_w']
    lng = p['sgu_ln_g'].reshape(1, G)
    lnb = p['sgu_ln_b'].reshape(1, G)
    sguw = p['sgu_w']
    sgubt = p['sgu_b'].T
    poolw = p['pool_w'].astype(BF16)
    pools = p['pool_scale'].reshape(1, G)
    outs = pl.pallas_call(
        functools.partial(_local_mixers_kernel, tb=tb),
        grid=(T // tb,),
        in_specs=[col(0), col(1), col(2), col(6), col(7), col(8),
                  full(convw), full(lng), full(lnb), full(sguw), full(sgubt), full(poolw), full(pools)],
        out_specs=[pl.BlockSpec((tb, G), lambda i: (i, 0))] * 3,
        out_shape=[jax.ShapeDtypeStruct((T, G), BF16)] * 3,
        scratch_shapes=[pltpu.VMEM((tb + CONV_HALO, G), F32), pltpu.VMEM((tb + POOL_HALO, G), F32)],
        compiler_params=_cparams(("arbitrary",)),
        name="local_mixers",
    )(proj, proj, proj, proj, proj, proj, convw, lng, lnb, sguw, sgubt, poolw, pools)
    assert n_sgu == sguw.shape[0]
    return outs


def _unit_lower_inverse(L, eye, row, col):
    blk16 = (row >> 4) == (col >> 4)
    blk32 = (row >> 5) == (col >> 5)
    Ld = jnp.where(blk16, L, 0.0)
    T = eye + Ld
    P = Ld
    for _ in range(3):
        P = _hdot(P, P)
        T = T + _hdot(T, P)
    Lo = jnp.where(blk32 & jnp.logical_not(blk16), L, 0.0)
    T = T + _hdot(T, _hdot(Lo, T))
    Lo = jnp.where(jnp.logical_not(blk32), L, 0.0)
    T = T + _hdot(T, _hdot(Lo, T))
    return T


def _wkv_chunk(r, lw, k, v, a, b, S):
    C = r.shape[0]
    row = lax.broadcasted_iota(jnp.int32, (C, C), 0)
    col = lax.broadcasted_iota(jnp.int32, (C, C), 1)
    incl = col <= row
    strict = col < row
    eye = jnp.where(row == col, 1.0, 0.0).astype(F32)
    lc = _hdot(jnp.where(incl, 1.0, 0.0).astype(F32), lw)
    lx = lc - lw
    e_neg = jnp.exp(-lc)
    at = a * jnp.exp(lx)
    rt = r * jnp.exp(lc)
    bt = b * e_neg
    kt = k * e_neg
    Lab = jnp.where(strict, _hdot_nt(at, bt), 0.0)
    Mak = jnp.where(strict, _hdot_nt(at, kt), 0.0)
    Mrb = jnp.where(incl, _hdot_nt(rt, bt), 0.0)
    Mrk = jnp.where(incl, _hdot_nt(rt, kt), 0.0)
    Tinv = _unit_lower_inverse(Lab, eye, row, col)
    W = _hdot(Tinv, at)
    U = _hdot(Tinv, _hdot(Mak, v))
    sa = _hdot_nt(W, S) + U
    y = _hdot_nt(rt, S) + _hdot(Mrb, sa) + _hdot(Mrk, v)
    e_end = jnp.exp(lc[C - 1:C, :] - lc)
    S_new = S * jnp.exp(lc[C - 1:C, :]) + _hdot_tn(sa, b * e_end) + _hdot_tn(v, k * e_end)
    return y, S_new


def _rwkv_kernel(*refs, tb, has_vres):
    if has_vres:
        (r_ref, k_ref, v_ref, sm_ref, vfirst_ref,
         mur_ref, muk_ref, muv_ref, mus_ref,
         w0_ref, wup_ref, a0_ref, aup_ref, gup_ref, v0_ref, vup_ref,
         kk_ref, ka_ref, rk_ref, lng_ref, lnb_ref,
         y_ref,
         prev_ref, prevs_ref, S_ref, stage_ref, ybuf_ref) = refs
    else:
        (r_ref, k_ref, v_ref, sm_ref,
         mur_ref, muk_ref, muv_ref, mus_ref,
         w0_ref, wup_ref, a0_ref, aup_ref, gup_ref,
         kk_ref, ka_ref, rk_ref, lng_ref, lnb_ref,
         y_ref, vfirst_out_ref,
         prev_ref, prevs_ref, S_ref, stage_ref, ybuf_ref) = refs
    t = pl.program_id(1)

    @pl.when(t == 0)
    def _():
        prev_ref[...] = jnp.zeros(prev_ref.shape, F32)
        prevs_ref[...] = jnp.zeros(prevs_ref.shape, F32)
        S_ref[...] = jnp.zeros(S_ref.shape, F32)

    def shift_mix(raw, prev_row, mu):
        first = lax.broadcasted_iota(jnp.int32, raw.shape, 0) == 0
        sh = jnp.where(first, prev_row, pltpu.roll(raw, 1, 0))
        return raw + (sh - raw) * mu

    r_raw, k_raw, v_raw, sm_raw = r_ref[...], k_ref[...], v_ref[...], sm_ref[...]
    r = shift_mix(r_raw, prev_ref[0, 7:8, :], mur_ref[...])
    k = shift_mix(k_raw, prev_ref[1, 7:8, :], muk_ref[...])
    v = shift_mix(v_raw, prev_ref[2, 7:8, :], muv_ref[...])
    sm = shift_mix(sm_raw, prevs_ref[7:8, :], mus_ref[...])
    prev_ref[0] = r_raw[tb - 8:tb, :]
    prev_ref[1] = k_raw[tb - 8:tb, :]
    prev_ref[2] = v_raw[tb - 8:tb, :]
    prevs_ref[...] = sm_raw[tb - 8:tb, :]

    dl, al, gl = wup_ref.shape[0], aup_ref.shape[0], gup_ref.shape[0]
    wd = sm[:, 0:dl]
    ad = sm[:, dl:dl + al]
    gd = sm[:, dl + al:dl + al + gl]
    w = -_softplus(-(w0_ref[...] + _bdot(jnp.tanh(wd), wup_ref[...]))) - 0.5
    lw = -jnp.exp(w)
    a = _sigmoid(a0_ref[...] + _bdot(ad, aup_ref[...]))
    g = _bdot(_sigmoid(gd), gup_ref[...])
    if has_vres:
        ml = vup_ref.shape[0]
        vd = sm[:, dl + al + gl:dl + al + gl + ml]
        v = v + (vfirst_ref[...] - v) * _sigmoid(v0_ref[...] + _bdot(vd, vup_ref[...]))
    else:
        vfirst_out_ref[...] = v

    li = lax.broadcasted_iota(jnp.int32, (LANES, LANES), 0) // RWKV_HEAD
    lj = lax.broadcasted_iota(jnp.int32, (LANES, LANES), 1) // RWKV_HEAD
    head_ones = jnp.where(li == lj, 1.0, 0.0).astype(F32)
    head_sum = lambda z: _hdot(z, head_ones)

    kk = k * kk_ref[...]
    kk = kk / jnp.maximum(jnp.sqrt(head_sum(kk * kk)), 1e-12)
    k = k * (1.0 + (a - 1.0) * ka_ref[...])

    stage_ref[0] = r
    stage_ref[1] = lw
    stage_ref[2] = k
    stage_ref[3] = v
    stage_ref[4] = -kk
    stage_ref[5] = kk * a

    def chunk_body(c, carry):
        rs = pl.ds(pl.multiple_of(c * WKV_CHUNK, WKV_CHUNK), WKV_CHUNK)
        tiles = [stage_ref[q, rs, :] for q in range(6)]
        ys = []
        for hd in range(LANES // RWKV_HEAD):
            ls = slice(hd * RWKV_HEAD, (hd + 1) * RWKV_HEAD)
            y_h, S_new = _wkv_chunk(*[z[:, ls] for z in tiles], S_ref[hd])
            S_ref[hd] = S_new
            ys.append(y_h)
        ybuf_ref[rs, :] = jnp.concatenate(ys, axis=1)
        return carry

    lax.fori_loop(0, tb // WKV_CHUNK, chunk_body, 0)

    y = ybuf_ref[...]
    inv_n = 1.0 / RWKV_HEAD
    mu = head_sum(y) * inv_n
    yc = y - mu
    var = head_sum(yc * yc) * inv_n
    y = yc * lax.rsqrt(var + RWKV_GN_EPS) * lng_ref[...] + lnb_ref[...]
    y = y + head_sum(r * k * rk_ref[...]) * v
    y_ref[...] = (y * g).astype(y_ref.dtype)


def _rwkv_mixer(proj, G, small_w, small_off, p, v_first):
    T = proj.shape[0]
    has_vres = v_first is not None
    tb = _pick(T, (256, 128, 64))
    n_pairs = G // LANES
    cb = G // LANES
    dl, al, gl = p['decay_up'].shape[0], p['iclr_up'].shape[0], p['gate_up'].shape[0]
    ml = p['vres_up'].shape[0] if has_vres else 0
    mu = p['shift_mu']
    mu_r, mu_k, mu_v = (mu[q * G:(q + 1) * G].reshape(1, G) for q in range(3))
    mu_s = jnp.pad(mu[3 * G:], (0, small_w - (dl + al + gl + ml))).reshape(1, small_w)

    def colblk(first):
        return pl.BlockSpec((tb, LANES), lambda q, t, f=first: (t, f + q))

    def vec(arr):
        return arr.reshape(1, G), pl.BlockSpec((1, LANES), lambda q, t: (0, q))

    def up(arr):
        return arr.astype(BF16), pl.BlockSpec((arr.shape[0], LANES), lambda q, t: (0, q))

    small_spec = pl.BlockSpec((tb, small_w), lambda q, t: (t, small_off // small_w))
    pair_spec = pl.BlockSpec((tb, LANES), lambda q, t: (t, q))
    mu_spec = pl.BlockSpec((1, LANES), lambda q, t: (0, q))
    args = [proj, proj, proj, proj]
    specs = [colblk(3 * cb), colblk(4 * cb), colblk(5 * cb), small_spec]
    if has_vres:
        args.append(v_first)
        specs.append(pair_spec)
    args += [mu_r, mu_k, mu_v, mu_s]
    specs += [mu_spec, mu_spec, mu_spec, pl.BlockSpec((1, small_w), lambda q, t: (0, 0))]
    names = ['decay_w0', 'decay_up', 'iclr_a0', 'iclr_up', 'gate_up']
    if has_vres:
        names += ['vres_v0', 'vres_up']
    names += ['k_k', 'k_a', 'r_k', 'lnx_g', 'lnx_b']
    for nm in names:
        arr, spec = up(p[nm]) if nm.endswith('_up') else vec(p[nm])
        args.append(arr)
        specs.append(spec)
    out_shape = [jax.ShapeDtypeStruct((T, G), BF16)]
    out_specs = [pair_spec]
    if not has_vres:
        out_shape.append(jax.ShapeDtypeStruct((T, G), F32))
        out_specs.append(pair_spec)
    outs = pl.pallas_call(
        functools.partial(_rwkv_kernel, tb=tb, has_vres=has_vres),
        grid=(n_pairs, T // tb),
        in_specs=specs,
        out_specs=out_specs,
        out_shape=out_shape,
        scratch_shapes=[pltpu.VMEM((3, 8, LANES), F32), pltpu.VMEM((8, small_w), F32),
                        pltpu.VMEM((LANES // RWKV_HEAD, RWKV_HEAD, RWKV_HEAD), F32),
                        pltpu.VMEM((6, tb, LANES), F32), pltpu.VMEM((tb, LANES), F32)],
        compiler_params=_cparams(("arbitrary", "arbitrary")),
        name="rwkv7_mixer",
    )(*args)
    if has_vres:
        return outs[0], v_first
    return outs[0], outs[1]


def _outproj_kernel(y0_ref, y1_ref, y2_ref, y3_ref, w_ref, x_ref, g_ref, b_ref, o_ref, *, splits, tn):
    kstep = pl.program_id(1)
    y_refs = (y0_ref, y1_ref, y2_ref, y3_ref)
    tk = w_ref.shape[0]
    n_steps = len(y_refs) * splits

    @pl.when(kstep == 0)
    def _():
        o_ref[...] = DEEPNORM_ALPHA * x_ref[...]

    for s in range(n_steps):
        @pl.when(kstep == s)
        def _(s=s):
            lhs = y_refs[s // splits][:, (s % splits) * tk:(s % splits + 1) * tk]
            for n in range(0, o_ref.shape[1], tn):
                o_ref[:, n:n + tn] += jnp.dot(lhs, w_ref[:, n:n + tn], preferred_element_type=F32)

    @pl.when(kstep == n_steps - 1)
    def _():
        o_ref[...] = _layer_norm(o_ref[...], g_ref[...], b_ref[...], LN_EPS)


def _outproj_ln(ys, w, x, g, b):
    T, D = x.shape
    G = ys[0].shape[1]
    tm = _pick(T, (512, 256, 128))
    tk = _pick(G, (512, 256, 128))
    splits = G // tk
    yspec = pl.BlockSpec((tm, G), lambda i, k: (i, 0))
    row = pl.BlockSpec((tm, D), lambda i, k: (i, 0))
    row_once = pl.BlockSpec((tm, D), lambda i, k: (i, 0), pipeline_mode=pl.Buffered(1))
    vec = pl.BlockSpec((1, D), lambda i, k: (0, 0))
    return pl.pallas_call(
        functools.partial(_outproj_kernel, splits=splits, tn=_pick(D, (512, 256, 128))),
        grid=(T // tm, len(ys) * splits),
        in_specs=[yspec] * 4 + [pl.BlockSpec((tk, D), lambda i, k: (k, 0)), row_once, vec, vec],
        out_specs=row,
        out_shape=jax.ShapeDtypeStruct((T, D), F32),
        compiler_params=_cparams(("arbitrary", "arbitrary")),
        name="outproj_ln",
    )(*ys, w, x, g.reshape(1, D), b.reshape(1, D))


def _ffn_kernel(x_ref, wg_ref, wu_ref, wd_ref, g_ref, b_ref, o_ref, xb_ref, *, tn):
    f = pl.program_id(1)
    nf = pl.num_programs(1)

    @pl.when(f == 0)
    def _():
        xb_ref[...] = x_ref[...].astype(BF16)
        o_ref[...] = DEEPNORM_ALPHA * x_ref[...]

    xb = xb_ref[...]
    gate = jnp.dot(xb, wg_ref[...], preferred_element_type=F32)
    upv = jnp.dot(xb, wu_ref[...], preferred_element_type=F32)
    hid = (gate * _sigmoid(gate) * upv).astype(BF16)
    for n in range(0, o_ref.shape[1], tn):
        o_ref[:, n:n + tn] += jnp.dot(hid, wd_ref[:, n:n + tn], preferred_element_type=F32)

    @pl.when(f == nf - 1)
    def _():
        o_ref[...] = _layer_norm(o_ref[...], g_ref[...], b_ref[...], LN_EPS)


def _ffn_ln(x, wg, wu, wd, g, b):
    T, D = x.shape
    F = wg.shape[1]
    tm = _pick(T, (512, 256, 128))
    tf = _pick(F, (256, 128))
    row = pl.BlockSpec((tm, D), lambda i, f: (i, 0))
    row_once = pl.BlockSpec((tm, D), lambda i, f: (i, 0), pipeline_mode=pl.Buffered(1))
    vec = pl.BlockSpec((1, D), lambda i, f: (0, 0))
    return pl.pallas_call(
        functools.partial(_ffn_kernel, tn=_pick(D, (512, 256, 128))),
        grid=(T // tm, F // tf),
        in_specs=[row_once, pl.BlockSpec((D, tf), lambda i, f: (0, f)), pl.BlockSpec((D, tf), lambda i, f: (0, f)),
                  pl.BlockSpec((tf, D), lambda i, f: (f, 0)), vec, vec],
        out_specs=row,
        out_shape=jax.ShapeDtypeStruct((T, D), F32),
        scratch_shapes=[pltpu.VMEM((tm, D), BF16)],
        compiler_params=_cparams(("arbitrary", "arbitrary")),
        name="ffn_ln",
    )(x, wg, wu, wd, g.reshape(1, D), b.reshape(1, D))


def _hybrid_layer(x, v_first, p):
    T, D = x.shape
    G = D // 4
    w_in = p['w_in']
    n_small = w_in.shape[1] - 9 * G
    small_w = _round_up(n_small, 2 * LANES)
    assert (9 * G) % small_w == 0
    w_perm = jnp.concatenate(
        [w_in[:, :6 * G], w_in[:, 6 * G + n_small:], w_in[:, 6 * G:6 * G + n_small],
         jnp.zeros((D, small_w - n_small), w_in.dtype)], axis=1).astype(BF16)
    proj = _inproj(x, w_perm)
    y_conv, y_sgu, y_pool = _local_mixers(proj, G, p)
    y_rwkv, v_first = _rwkv_mixer(proj, G, small_w, 9 * G, p, v_first)
    x = _outproj_ln((y_conv, y_rwkv, y_sgu, y_pool), p['w_out'].astype(BF16), x, p['ln_mix_g'], p['ln_mix_b'])
    x = _ffn_ln(x, p['ffn_gate'].astype(BF16), p['ffn_up'].astype(BF16), p['ffn_down'].astype(BF16),
                p['ln_ffn_g'], p['ln_ffn_b'])
    return x, v_first


_NAMES_0 = ('w_in', 'conv_w', 'shift_mu', 'decay_w0', 'decay_up', 'iclr_a0', 'iclr_up', 'gate_up',
            'k_k', 'k_a', 'r_k', 'lnx_g', 'lnx_b', 'sgu_ln_g', 'sgu_ln_b', 'sgu_w', 'sgu_b',
            'pool_w', 'pool_scale', 'w_out', 'ln_mix_g', 'ln_mix_b', 'ffn_gate', 'ffn_up', 'ffn_down',
            'ln_ffn_g', 'ln_ffn_b')
_NAMES_1 = _NAMES_0[:7] + ('vres_v0', 'vres_up') + _NAMES_0[7:]


def kernel(x, w_in_0, conv_w_0, shift_mu_0, decay_w0_0, decay_up_0, iclr_a0_0, iclr_up_0, gate_up_0, k_k_0, k_a_0, r_k_0, lnx_g_0, lnx_b_0, sgu_ln_g_0, sgu_ln_b_0, sgu_w_0, sgu_b_0, pool_w_0, pool_scale_0, w_out_0, ln_mix_g_0, ln_mix_b_0, ffn_gate_0, ffn_up_0, ffn_down_0, ln_ffn_g_0, ln_ffn_b_0, w_in_1, conv_w_1, shift_mu_1, decay_w0_1, decay_up_1, iclr_a0_1, iclr_up_1, vres_v0_1, vres_up_1, gate_up_1, k_k_1, k_a_1, r_k_1, lnx_g_1, lnx_b_1, sgu_ln_g_1, sgu_ln_b_1, sgu_w_1, sgu_b_1, pool_w_1, pool_scale_1, w_out_1, ln_mix_g_1, ln_mix_b_1, ffn_gate_1, ffn_up_1, ffn_down_1, ln_ffn_g_1, ln_ffn_b_1):
    p0 = dict(zip(_NAMES_0, (w_in_0, conv_w_0, shift_mu_0, decay_w0_0, decay_up_0, iclr_a0_0, iclr_up_0, gate_up_0, k_k_0, k_a_0, r_k_0, lnx_g_0, lnx_b_0, sgu_ln_g_0, sgu_ln_b_0, sgu_w_0, sgu_b_0, pool_w_0, pool_scale_0, w_out_0, ln_mix_g_0, ln_mix_b_0, ffn_gate_0, ffn_up_0, ffn_down_0, ln_ffn_g_0, ln_ffn_b_0)))
    p1 = dict(zip(_NAMES_1, (w_in_1, conv_w_1, shift_mu_1, decay_w0_1, decay_up_1, iclr_a0_1, iclr_up_1, vres_v0_1, vres_up_1, gate_up_1, k_k_1, k_a_1, r_k_1, lnx_g_1, lnx_b_1, sgu_ln_g_1, sgu_ln_b_1, sgu_w_1, sgu_b_1, pool_w_1, pool_scale_1, w_out_1, ln_mix_g_1, ln_mix_b_1, ffn_gate_1, ffn_up_1, ffn_down_1, ln_ffn_g_1, ln_ffn_b_1)))
    B, T, D = x.shape
    assert B == 1
    h = x.reshape(T, D)
    h, v_first = _hybrid_layer(h, None, p0)
    h, _ = _hybrid_layer(h, v_first, p1)
    return h.reshape(B, T, D)
```

```python
import functools

import jax
import jax.numpy as jnp
from jax import lax
from jax.experimental import pallas as pl
from jax.experimental.pallas import tpu as pltpu

LANES = 128
RWKV_HEAD = 64
SGU_CHUNK = 128
SGU_HEAD = 128
CONV_W = 3
POOL_WINDOWS = (2, 4, 8, 16)
POOL_HALO = 16
CONV_HALO = 8
LN_EPS = 1e-5
RWKV_GN_EPS = 64e-5
WKV_CHUNK = 64
DEPTH = 2
DEEPNORM_ALPHA = (2 * DEPTH) ** 0.25
VMEM_LIMIT_BYTES = 60 * 1024 * 1024

F32 = jnp.float32
BF16 = jnp.bfloat16
HI = lax.Precision.HIGHEST


def _cparams(sem):
    return pltpu.CompilerParams(dimension_semantics=sem, vmem_limit_bytes=VMEM_LIMIT_BYTES)


def _pick(n, prefs):
    for p in prefs:
        if n % p == 0:
            return p
    return n


def _round_up(n, m):
    return (n + m - 1) // m * m


def _bdot(a, b):
    return jnp.dot(a.astype(BF16), b.astype(BF16), preferred_element_type=F32)


def _hdot(a, b):
    return jnp.dot(a, b, precision=HI, preferred_element_type=F32)


def _layer_norm(x, g, b, eps):
    mu = jnp.mean(x, axis=-1, keepdims=True)
    xc = x - mu
    var = jnp.mean(xc * xc, axis=-1, keepdims=True)
    return xc * lax.rsqrt(var + eps) * g + b


def _sigmoid(x):
    return 1.0 / (1.0 + jnp.exp(-x))


def _gelu_tanh(x):
    c = 0.7978845608028654
    return 0.5 * x * (1.0 + jnp.tanh(c * (x + 0.044715 * (x * x * x))))


def _softplus(x):
    return jnp.maximum(x, 0.0) + jnp.log(1.0 + jnp.exp(-jnp.abs(x)))


def _inproj_kernel(x_ref, w_ref, o_ref, xb_ref):
    @pl.when(pl.program_id(1) == 0)
    def _():
        xb_ref[...] = x_ref[...].astype(BF16)

    o_ref[...] = jnp.dot(xb_ref[...], w_ref[...], preferred_element_type=F32)


def _inproj(x, w):
    T, D = x.shape
    N = w.shape[1]
    tm = _pick(T, (512, 256, 128))
    tn = _pick(N, (512, 256, 128))
    return pl.pallas_call(
        _inproj_kernel,
        grid=(T // tm, N // tn),
        in_specs=[pl.BlockSpec((tm, D), lambda i, j: (i, 0)),
                  pl.BlockSpec((D, tn), lambda i, j: (0, j))],
        out_specs=pl.BlockSpec((tm, tn), lambda i, j: (i, j)),
        out_shape=jax.ShapeDtypeStruct((T, N), F32),
        scratch_shapes=[pltpu.VMEM((tm, D), BF16)],
        compiler_params=_cparams(("arbitrary", "arbitrary")),
        name="inproj",
    )(x, w)


def _local_mixers_kernel(h_ref, bg_ref, cg_ref, su_ref, sv_ref, pz_ref,
                         convw_ref, lng_ref, lnb_ref, sguw_ref, sgubt_ref, poolw_ref, pools_ref,
                         yconv_ref, ysgu_ref, ypool_ref,
                         zbuf_ref, pbuf_ref, *, tb):
    i = pl.program_id(0)

    @pl.when(i == 0)
    def _():
        zbuf_ref[0:CONV_HALO, :] = jnp.zeros((CONV_HALO, zbuf_ref.shape[1]), F32)
        pbuf_ref[0:POOL_HALO, :] = jnp.zeros((POOL_HALO, pbuf_ref.shape[1]), F32)

    zbuf_ref[CONV_HALO:CONV_HALO + tb, :] = cg_ref[...] * h_ref[...]
    conv = zbuf_ref[CONV_HALO - (CONV_W - 1):CONV_HALO - (CONV_W - 1) + tb, :] * convw_ref[0:1, :]
    for j in range(1, CONV_W):
        off = CONV_HALO - (CONV_W - 1 - j)
        conv = conv + zbuf_ref[off:off + tb, :] * convw_ref[j:j + 1, :]
    yconv_ref[...] = (bg_ref[...] * conv).astype(yconv_ref.dtype)
    zbuf_ref[0:CONV_HALO, :] = zbuf_ref[tb:tb + CONV_HALO, :]

    u = _gelu_tanh(su_ref[...])
    v = _layer_norm(_gelu_tanh(sv_ref[...]), lng_ref[...], lnb_ref[...], LN_EPS).astype(BF16)
    n_heads = sguw_ref.shape[0]
    row = lax.broadcasted_iota(jnp.int32, (SGU_CHUNK, SGU_CHUNK), 0)
    col = lax.broadcasted_iota(jnp.int32, (SGU_CHUNK, SGU_CHUNK), 1)
    causal = col <= row
    for hd in range(n_heads):
        w_h = jnp.where(causal, sguw_ref[hd], 0.0).astype(BF16)
        bias = sgubt_ref[:, hd:hd + 1]
        cs = slice(hd * SGU_HEAD, (hd + 1) * SGU_HEAD)
        for c in range(tb // SGU_CHUNK):
            rs = slice(c * SGU_CHUNK, (c + 1) * SGU_CHUNK)
            s = jnp.dot(w_h, v[rs, cs], preferred_element_type=F32) + bias
            ysgu_ref[rs, cs] = (u[rs, cs] * s).astype(ysgu_ref.dtype)

    pbuf_ref[POOL_HALO:POOL_HALO + tb, :] = pz_ref[...]
    pg = poolw_ref.shape[1]
    t_glob = i * tb + lax.broadcasted_iota(jnp.int32, (tb, 1), 0)
    for gi, win in enumerate(POOL_WINDOWS):
        cs = slice(gi * pg, (gi + 1) * pg)
        z = pbuf_ref[POOL_HALO:POOL_HALO + tb, cs]
        acc = z
        for j in range(1, win):
            acc = acc + pbuf_ref[POOL_HALO - j:POOL_HALO - j + tb, cs]
        cnt = jnp.minimum(t_glob + 1, win).astype(F32)
        d = acc / cnt - z
        y = _bdot(d, poolw_ref[gi])
        ypool_ref[:, cs] = (y * pools_ref[:, cs]).astype(ypool_ref.dtype)
    pbuf_ref[0:POOL_HALO, :] = pbuf_ref[tb:tb + POOL_HALO, :]


def _local_mixers(proj, G, p):
    T = proj.shape[0]
    tb = _pick(T, (256, 128))
    n_sgu = G // SGU_HEAD
    col = lambda c: pl.BlockSpec((tb, G), lambda i, c=c: (i, c))
    full = lambda a: pl.BlockSpec(a.shape, lambda i, n=a.ndim: (0,) * n)
    convw = p['conv_w']
    lng = p['sgu_ln_g'].reshape(1, G)
    lnb = p['sgu_ln_b'].reshape(1, G)
    sguw = p['sgu_w']
    sgubt = p['sgu_b'].T
    poolw = p['pool_w'].astype(BF16)
    pools = p['pool_scale'].reshape(1, G)
    outs = pl.pallas_call(
        functools.partial(_local_mixers_kernel, tb=tb),
        grid=(T // tb,),
        in_specs=[col(0), col(1), col(2), col(6), col(7), col(8),
                  full(convw), full(lng), full(lnb), full(sguw), full(sgubt), full(poolw), full(pools)],
        out_specs=[pl.BlockSpec((tb, G), lambda i: (i, 0))] * 3,
        out_shape=[jax.ShapeDtypeStruct((T, G), BF16)] * 3,
        scratch_shapes=[pltpu.VMEM((tb + CONV_HALO, G), F32), pltpu.VMEM((tb + POOL_HALO, G), F32)],
        compiler_params=_cparams(("arbitrary",)),
        name="local_mixers",
    )(proj, proj, proj, proj, proj, proj, convw, lng, lnb, sguw, sgubt, poolw, pools)
    assert n_sgu == sguw.shape[0]
    return outs


def _each(fn, *lists):
    return [fn(*args) for args in zip(*lists)]


def _unit_lower_inverse(Ls, eye, row, col):
    blk16 = (row >> 4) == (col >> 4)
    blk32 = (row >> 5) == (col >> 5)
    P = [jnp.where(blk16, L, 0.0) for L in Ls]
    T = [eye + p for p in P]
    for _ in range(3):
        P = _each(_bdot, P, P)
        T = _each(lambda t, p: t + _bdot(t, p), T, P)
    for off_diag in (blk32 & jnp.logical_not(blk16), jnp.logical_not(blk32)):
        X = _each(lambda L, t: _bdot(jnp.where(off_diag, L, 0.0), t), Ls, T)
        T = _each(lambda t, x: t + _bdot(t, x), T, X)
    return T


def _bdot_nt(a, b):
    return lax.dot_general(a.astype(BF16), b.astype(BF16), (((1,), (1,)), ((), ())), preferred_element_type=F32)


def _bdot_tn(a, b):
    return lax.dot_general(a.astype(BF16), b.astype(BF16), (((0,), (0,)), ((), ())), preferred_element_type=F32)


def _wkv_chunk_operators(at, rt, bt, kt, bh, kh, v):
    C = at[0].shape[0]
    row = lax.broadcasted_iota(jnp.int32, (C, C), 0)
    col = lax.broadcasted_iota(jnp.int32, (C, C), 1)
    incl = col <= row
    strict = col < row
    eye = jnp.where(row == col, 1.0, 0.0).astype(F32)
    ar = _each(lambda x, y: jnp.concatenate([x, y.astype(BF16)], axis=0), at, rt)
    p_b = _each(_bdot_nt, ar, bt)
    p_k = _each(_bdot_nt, ar, kt)
    Lab = [jnp.where(strict, p[:C], 0.0) for p in p_b]
    Mak = [jnp.where(strict, p[:C], 0.0) for p in p_k]
    Mrb = [jnp.where(incl, p[C:], 0.0).astype(BF16) for p in p_b]
    Mrk = [jnp.where(incl, p[C:], 0.0) for p in p_k]
    MakV = _each(_bdot, Mak, v)
    MrkV = _each(_bdot, Mrk, v)
    KV = _each(_bdot_tn, v, kh)
    Tinv = [t.astype(BF16) for t in _unit_lower_inverse(Lab, eye, row, col)]
    W = [w.astype(BF16) for w in _each(_bdot, Tinv, at)]
    U = [u.astype(BF16) for u in _each(_bdot, Tinv, MakV)]
    Q = _each(lambda x, m, w: x + _bdot(m, w), rt, Mrb, W)
    y_add = _each(lambda m, u, mv: _bdot(m, u) + mv, Mrb, U, MrkV)
    m_state = _each(_bdot_tn, W, bh)
    s_add = _each(lambda u, b_, kv: _bdot_tn(u, b_) + kv, U, bh, KV)
    return Q, y_add, m_state, s_add


def _rwkv_kernel(*refs, tb, has_vres):
    if has_vres:
        (r_ref, k_ref, v_ref, sm_ref, vfirst_ref,
         mur_ref, muk_ref, muv_ref, mus_ref,
         w0_ref, wup_ref, a0_ref, aup_ref, gup_ref, v0_ref, vup_ref,
         kk_ref, ka_ref, rk_ref, lng_ref, lnb_ref,
         y_ref,
         prev_ref, prevs_ref, S_ref, ybuf_ref) = refs
    else:
        (r_ref, k_ref, v_ref, sm_ref,
         mur_ref, muk_ref, muv_ref, mus_ref,
         w0_ref, wup_ref, a0_ref, aup_ref, gup_ref,
         kk_ref, ka_ref, rk_ref, lng_ref, lnb_ref,
         y_ref, vfirst_out_ref,
         prev_ref, prevs_ref, S_ref, ybuf_ref) = refs
    t = pl.program_id(1)

    @pl.when(t == 0)
    def _():
        prev_ref[...] = jnp.zeros(prev_ref.shape, F32)
        prevs_ref[...] = jnp.zeros(prevs_ref.shape, F32)
        S_ref[...] = jnp.zeros(S_ref.shape, F32)

    def shift_mix(raw, prev_row, mu):
        first = lax.broadcasted_iota(jnp.int32, raw.shape, 0) == 0
        sh = jnp.where(first, prev_row, pltpu.roll(raw, 1, 0))
        return raw + (sh - raw) * mu

    r_raw, k_raw, v_raw, sm_raw = r_ref[...], k_ref[...], v_ref[...], sm_ref[...]
    r = shift_mix(r_raw, prev_ref[0, 7:8, :], mur_ref[...])
    k = shift_mix(k_raw, prev_ref[1, 7:8, :], muk_ref[...])
    v = shift_mix(v_raw, prev_ref[2, 7:8, :], muv_ref[...])
    sm = shift_mix(sm_raw, prevs_ref[7:8, :], mus_ref[...])
    prev_ref[0] = r_raw[tb - 8:tb, :]
    prev_ref[1] = k_raw[tb - 8:tb, :]
    prev_ref[2] = v_raw[tb - 8:tb, :]
    prevs_ref[...] = sm_raw[tb - 8:tb, :]

    dl, al, gl = wup_ref.shape[0], aup_ref.shape[0], gup_ref.shape[0]
    wd = sm[:, 0:dl]
    ad = sm[:, dl:dl + al]
    gd = sm[:, dl + al:dl + al + gl]
    w = -_softplus(-(w0_ref[...] + _bdot(jnp.tanh(wd), wup_ref[...]))) - 0.5
    lw = -jnp.exp(w)
    a = _sigmoid(a0_ref[...] + _bdot(ad, aup_ref[...]))
    g = _bdot(_sigmoid(gd), gup_ref[...])
    if has_vres:
        ml = vup_ref.shape[0]
        vd = sm[:, dl + al + gl:dl + al + gl + ml]
        v = v + (vfirst_ref[...] - v) * _sigmoid(v0_ref[...] + _bdot(vd, vup_ref[...]))
    else:
        vfirst_out_ref[...] = v

    li = lax.broadcasted_iota(jnp.int32, (LANES, LANES), 0) // RWKV_HEAD
    lj = lax.broadcasted_iota(jnp.int32, (LANES, LANES), 1) // RWKV_HEAD
    head_ones = jnp.where(li == lj, 1.0, 0.0).astype(F32)
    head_sum = lambda z: _hdot(z, head_ones)

    kk = k * kk_ref[...]
    kk = kk / jnp.maximum(jnp.sqrt(head_sum(kk * kk)), 1e-12)
    k = k * (1.0 + (a - 1.0) * ka_ref[...])

    C = WKV_CHUNK
    n_chunks = tb // C
    ti = lax.broadcasted_iota(jnp.int32, (tb, tb), 0)
    tj = lax.broadcasted_iota(jnp.int32, (tb, tb), 1)
    tri = jnp.where((tj <= ti) & ((ti // C) == (tj // C)), 1.0, 0.0).astype(BF16)
    lw_hi = lw.astype(BF16)
    lw_r1 = lw - lw_hi.astype(F32)
    lw_mid = lw_r1.astype(BF16)
    lw_lo = (lw_r1 - lw_mid.astype(F32)).astype(BF16)
    lc = (jnp.dot(tri, lw_hi, preferred_element_type=F32) + jnp.dot(tri, lw_mid, preferred_element_type=F32)
          + jnp.dot(tri, lw_lo, preferred_element_type=F32))
    lc_end = jnp.concatenate(
        [jnp.broadcast_to(lc[(c + 1) * C - 1:(c + 1) * C, :], (C, LANES)) for c in range(n_chunks)], axis=0)
    e_neg = jnp.exp(-lc)
    e_end = jnp.exp(lc_end - lc)
    at = (-kk * jnp.exp(lc - lw)).astype(BF16)
    rt = r * jnp.exp(lc)
    bt = (kk * a * e_neg).astype(BF16)
    kt = (k * e_neg).astype(BF16)
    bh = (kk * a * e_end).astype(BF16)
    kh = (k * e_end).astype(BF16)
    vb = v.astype(BF16)
    decay_end = jnp.exp(lc_end)

    n_heads = LANES // RWKV_HEAD
    tiles = [(c, hd) for c in range(n_chunks) for hd in range(n_heads)]
    cut = lambda z: [z[c * C:(c + 1) * C, hd * RWKV_HEAD:(hd + 1) * RWKV_HEAD] for c, hd in tiles]
    Q, y_add, m_state, s_add = _wkv_chunk_operators(*[cut(z) for z in (at, rt, bt, kt, bh, kh, vb)])
    S = [S_ref[hd] for hd in range(n_heads)]
    for c in range(n_chunks):
        for hd in range(n_heads):
            i = c * n_heads + hd
            ls = slice(hd * RWKV_HEAD, (hd + 1) * RWKV_HEAD)
            ybuf_ref[c * C:(c + 1) * C, ls] = _bdot_nt(Q[i], S[hd]) + y_add[i]
            S[hd] = S[hd] * decay_end[c * C:c * C + 1, ls] + _bdot(S[hd], m_state[i]) + s_add[i]
    for hd in range(n_heads):
        S_ref[hd] = S[hd]

    y = ybuf_ref[...]
    inv_n = 1.0 / RWKV_HEAD
    mu = head_sum(y) * inv_n
    yc = y - mu
    var = head_sum(yc * yc) * inv_n
    y = yc * lax.rsqrt(var + RWKV_GN_EPS) * lng_ref[...] + lnb_ref[...]
    y = y + head_sum(r * k * rk_ref[...]) * v
    y_ref[...] = (y * g).astype(y_ref.dtype)


def _rwkv_mixer(proj, G, small_w, small_off, p, v_first):
    T = proj.shape[0]
    has_vres = v_first is not None
    tb = _pick(T, (256, 128, 64))
    n_pairs = G // LANES
    cb = G // LANES
    dl, al, gl = p['decay_up'].shape[0], p['iclr_up'].shape[0], p['gate_up'].shape[0]
    ml = p['vres_up'].shape[0] if has_vres else 0
    mu = p['shift_mu']
    mu_r, mu_k, mu_v = (mu[q * G:(q + 1) * G].reshape(1, G) for q in range(3))
    mu_s = jnp.pad(mu[3 * G:], (0, small_w - (dl + al + gl + ml))).reshape(1, small_w)

    def colblk(first):
        return pl.BlockSpec((tb, LANES), lambda q, t, f=first: (t, f + q))

    def vec(arr):
        return arr.reshape(1, G), pl.BlockSpec((1, LANES), lambda q, t: (0, q))

    def up(arr):
        return arr.astype(BF16), pl.BlockSpec((arr.shape[0], LANES), lambda q, t: (0, q))

    small_spec = pl.BlockSpec((tb, small_w), lambda q, t: (t, small_off // small_w))
    pair_spec = pl.BlockSpec((tb, LANES), lambda q, t: (t, q))
    mu_spec = pl.BlockSpec((1, LANES), lambda q, t: (0, q))
    args = [proj, proj, proj, proj]
    specs = [colblk(3 * cb), colblk(4 * cb), colblk(5 * cb), small_spec]
    if has_vres:
        args.append(v_first)
        specs.append(pair_spec)
    args += [mu_r, mu_k, mu_v, mu_s]
    specs += [mu_spec, mu_spec, mu_spec, pl.BlockSpec((1, small_w), lambda q, t: (0, 0))]
    names = ['decay_w0', 'decay_up', 'iclr_a0', 'iclr_up', 'gate_up']
    if has_vres:
        names += ['vres_v0', 'vres_up']
    names += ['k_k', 'k_a', 'r_k', 'lnx_g', 'lnx_b']
    for nm in names:
        arr, spec = up(p[nm]) if nm.endswith('_up') else vec(p[nm])
        args.append(arr)
        specs.append(spec)
    out_shape = [jax.ShapeDtypeStruct((T, G), BF16)]
    out_specs = [pair_spec]
    if not has_vres:
        out_shape.append(jax.ShapeDtypeStruct((T, G), F32))
        out_specs.append(pair_spec)
    outs = pl.pallas_call(
        functools.partial(_rwkv_kernel, tb=tb, has_vres=has_vres),
        grid=(n_pairs, T // tb),
        in_specs=specs,
        out_specs=out_specs,
        out_shape=out_shape,
        scratch_shapes=[pltpu.VMEM((3, 8, LANES), F32), pltpu.VMEM((8, small_w), F32),
                        pltpu.VMEM((LANES // RWKV_HEAD, RWKV_HEAD, RWKV_HEAD), F32),
                        pltpu.VMEM((tb, LANES), F32)],
        compiler_params=_cparams(("arbitrary", "arbitrary")),
        name="rwkv7_mixer",
    )(*args)
    if has_vres:
        return outs[0], v_first
    return outs[0], outs[1]


def _outproj_kernel(y0_ref, y1_ref, y2_ref, y3_ref, w_ref, x_ref, g_ref, b_ref, o_ref, *, splits, tn):
    kstep = pl.program_id(1)
    y_refs = (y0_ref, y1_ref, y2_ref, y3_ref)
    tk = w_ref.shape[0]
    n_steps = len(y_refs) * splits

    @pl.when(kstep == 0)
    def _():
        o_ref[...] = DEEPNORM_ALPHA * x_ref[...]

    for s in range(n_steps):
        @pl.when(kstep == s)
        def _(s=s):
            lhs = y_refs[s // splits][:, (s % splits) * tk:(s % splits + 1) * tk]
            for n in range(0, o_ref.shape[1], tn):
                o_ref[:, n:n + tn] += jnp.dot(lhs, w_ref[:, n:n + tn], preferred_element_type=F32)

    @pl.when(kstep == n_steps - 1)
    def _():
        o_ref[...] = _layer_norm(o_ref[...], g_ref[...], b_ref[...], LN_EPS)


def _outproj_ln(ys, w, x, g, b):
    T, D = x.shape
    G = ys[0].shape[1]
    tm = _pick(T, (512, 256, 128))
    tk = _pick(G, (512, 256, 128))
    splits = G // tk
    yspec = pl.BlockSpec((tm, G), lambda i, k: (i, 0))
    row = pl.BlockSpec((tm, D), lambda i, k: (i, 0))
    row_once = pl.BlockSpec((tm, D), lambda i, k: (i, 0), pipeline_mode=pl.Buffered(1))
    vec = pl.BlockSpec((1, D), lambda i, k: (0, 0))
    return pl.pallas_call(
        functools.partial(_outproj_kernel, splits=splits, tn=_pick(D, (512, 256, 128))),
        grid=(T // tm, len(ys) * splits),
        in_specs=[yspec] * 4 + [pl.BlockSpec((tk, D), lambda i, k: (k, 0)), row_once, vec, vec],
        out_specs=row,
        out_shape=jax.ShapeDtypeStruct((T, D), F32),
        compiler_params=_cparams(("arbitrary", "arbitrary")),
        name="outproj_ln",
    )(*ys, w, x, g.reshape(1, D), b.reshape(1, D))


def _ffn_kernel(x_ref, wg_ref, wu_ref, wd_ref, g_ref, b_ref, o_ref, xb_ref, *, tn):
    f = pl.program_id(1)
    nf = pl.num_programs(1)

    @pl.when(f == 0)
    def _():
        xb_ref[...] = x_ref[...].astype(BF16)
        o_ref[...] = DEEPNORM_ALPHA * x_ref[...]

    xb = xb_ref[...]
    gate = jnp.dot(xb, wg_ref[...], preferred_element_type=F32)
    upv = jnp.dot(xb, wu_ref[...], preferred_element_type=F32)
    hid = (gate * _sigmoid(gate) * upv).astype(BF16)
    for n in range(0, o_ref.shape[1], tn):
        o_ref[:, n:n + tn] += jnp.dot(hid, wd_ref[:, n:n + tn], preferred_element_type=F32)

    @pl.when(f == nf - 1)
    def _():
        o_ref[...] = _layer_norm(o_ref[...], g_ref[...], b_ref[...], LN_EPS)


def _ffn_ln(x, wg, wu, wd, g, b):
    T, D = x.shape
    F = wg.shape[1]
    tm = _pick(T, (512, 256, 128))
    tf = _pick(F, (256, 128))
    row = pl.BlockSpec((tm, D), lambda i, f: (i, 0))
    row_once = pl.BlockSpec((tm, D), lambda i, f: (i, 0), pipeline_mode=pl.Buffered(1))
    vec = pl.BlockSpec((1, D), lambda i, f: (0, 0))
    return pl.pallas_call(
        functools.partial(_ffn_kernel, tn=_pick(D, (512, 256, 128))),
        grid=(T // tm, F // tf),
        in_specs=[row_once, pl.BlockSpec((D, tf), lambda i, f: (0, f)), pl.BlockSpec((D, tf), lambda i, f: (0, f)),
                  pl.BlockSpec((tf, D), lambda i, f: (f, 0)), vec, vec],
        out_specs=row,
        out_shape=jax.ShapeDtypeStruct((T, D), F32),
        scratch_shapes=[pltpu.VMEM((tm, D), BF16)],
        compiler_params=_cparams(("arbitrary", "arbitrary")),
        name="ffn_ln",
    )(x, wg, wu, wd, g.reshape(1, D), b.reshape(1, D))


def _hybrid_layer(x, v_first, p):
    T, D = x.shape
    G = D // 4
    w_in = p['w_in']
    n_small = w_in.shape[1] - 9 * G
    small_w = _round_up(n_small, 2 * LANES)
    assert (9 * G) % small_w == 0
    w_perm = jnp.concatenate(
        [w_in[:, :6 * G], w_in[:, 6 * G + n_small:], w_in[:, 6 * G:6 * G + n_small],
         jnp.zeros((D, small_w - n_small), w_in.dtype)], axis=1).astype(BF16)
    proj = _inproj(x, w_perm)
    y_conv, y_sgu, y_pool = _local_mixers(proj, G, p)
    y_rwkv, v_first = _rwkv_mixer(proj, G, small_w, 9 * G, p, v_first)
    x = _outproj_ln((y_conv, y_rwkv, y_sgu, y_pool), p['w_out'].astype(BF16), x, p['ln_mix_g'], p['ln_mix_b'])
    x = _ffn_ln(x, p['ffn_gate'].astype(BF16), p['ffn_up'].astype(BF16), p['ffn_down'].astype(BF16),
                p['ln_ffn_g'], p['ln_ffn_b'])
    return x, v_first


_NAMES_0 = ('w_in', 'conv_w', 'shift_mu', 'decay_w0', 'decay_up', 'iclr_a0', 'iclr_up', 'gate_up',
            'k_k', 'k_a', 'r_k', 'lnx_g', 'lnx_b', 'sgu_ln_g', 'sgu_ln_b', 'sgu_w', 'sgu_b',
            'pool_w', 'pool_scale', 'w_out', 'ln_mix_g', 'ln_mix_b', 'ffn_gate', 'ffn_up', 'ffn_down',
            'ln_ffn_g', 'ln_ffn_b')
_NAMES_1 = _NAMES_0[:7] + ('vres_v0', 'vres_up') + _NAMES_0[7:]


def kernel(x, w_in_0, conv_w_0, shift_mu_0, decay_w0_0, decay_up_0, iclr_a0_0, iclr_up_0, gate_up_0, k_k_0, k_a_0, r_k_0, lnx_g_0, lnx_b_0, sgu_ln_g_0, sgu_ln_b_0, sgu_w_0, sgu_b_0, pool_w_0, pool_scale_0, w_out_0, ln_mix_g_0, ln_mix_b_0, ffn_gate_0, ffn_up_0, ffn_down_0, ln_ffn_g_0, ln_ffn_b_0, w_in_1, conv_w_1, shift_mu_1, decay_w0_1, decay_up_1, iclr_a0_1, iclr_up_1, vres_v0_1, vres_up_1, gate_up_1, k_k_1, k_a_1, r_k_1, lnx_g_1, lnx_b_1, sgu_ln_g_1, sgu_ln_b_1, sgu_w_1, sgu_b_1, pool_w_1, pool_scale_1, w_out_1, ln_mix_g_1, ln_mix_b_1, ffn_gate_1, ffn_up_1, ffn_down_1, ln_ffn_g_1, ln_ffn_b_1):
    p0 = dict(zip(_NAMES_0, (w_in_0, conv_w_0, shift_mu_0, decay_w0_0, decay_up_0, iclr_a0_0, iclr_up_0, gate_up_0, k_k_0, k_a_0, r_k_0, lnx_g_0, lnx_b_0, sgu_ln_g_0, sgu_ln_b_0, sgu_w_0, sgu_b_0, pool_w_0, pool_scale_0, w_out_0, ln_mix_g_0, ln_mix_b_0, ffn_gate_0, ffn_up_0, ffn_down_0, ln_ffn_g_0, ln_ffn_b_0)))
    p1 = dict(zip(_NAMES_1, (w_in_1, conv_w_1, shift_mu_1, decay_w0_1, decay_up_1, iclr_a0_1, iclr_up_1, vres_v0_1, vres_up_1, gate_up_1, k_k_1, k_a_1, r_k_1, lnx_g_1, lnx_b_1, sgu_ln_g_1, sgu_ln_b_1, sgu_w_1, sgu_b_1, pool_w_1, pool_scale_1, w_out_1, ln_mix_g_1, ln_mix_b_1, ffn_gate_1, ffn_up_1, ffn_down_1, ln_ffn_g_1, ln_ffn_b_1)))
    B, T, D = x.shape
    assert B == 1
    h = x.reshape(T, D)
    h, v_first = _hybrid_layer(h, None, p0)
    h, _ = _hybrid_layer(h, v_first, p1)
    return h.reshape(B, T, D)
```

```python
import functools

import jax
import jax.numpy as jnp
from jax import lax
from jax.experimental import pallas as pl
from jax.experimental.pallas import tpu as pltpu

LANES = 128
RWKV_HEAD = 64
SGU_CHUNK = 128
SGU_HEAD = 128
CONV_W = 3
POOL_WINDOWS = (2, 4, 8, 16)
POOL_HALO = 16
CONV_HALO = 8
LN_EPS = 1e-5
RWKV_GN_EPS = 64e-5
WKV_CHUNK = 64
RWKV_BLOCK_LANES = 512
DEPTH = 2
DEEPNORM_ALPHA = (2 * DEPTH) ** 0.25
VMEM_LIMIT_BYTES = 60 * 1024 * 1024

F32 = jnp.float32
BF16 = jnp.bfloat16
HI = lax.Precision.HIGHEST


def _cparams(sem):
    return pltpu.CompilerParams(dimension_semantics=sem, vmem_limit_bytes=VMEM_LIMIT_BYTES)


def _pick(n, prefs):
    for p in prefs:
        if n % p == 0:
            return p
    return n


def _round_up(n, m):
    return (n + m - 1) // m * m


def _bdot(a, b):
    return jnp.dot(a.astype(BF16), b.astype(BF16), preferred_element_type=F32)


def _hdot(a, b):
    return jnp.dot(a, b, precision=HI, preferred_element_type=F32)


def _layer_norm(x, g, b, eps):
    mu = jnp.mean(x, axis=-1, keepdims=True)
    xc = x - mu
    var = jnp.mean(xc * xc, axis=-1, keepdims=True)
    return xc * lax.rsqrt(var + eps) * g + b


def _sigmoid(x):
    return 1.0 / (1.0 + jnp.exp(-x))


def _gelu_tanh(x):
    c = 0.7978845608028654
    return 0.5 * x * (1.0 + jnp.tanh(c * (x + 0.044715 * (x * x * x))))


def _softplus(x):
    return jnp.maximum(x, 0.0) + jnp.log(1.0 + jnp.exp(-jnp.abs(x)))


def _inproj_kernel(x_ref, w_ref, o_ref, xb_ref):
    @pl.when(pl.program_id(1) == 0)
    def _():
        xb_ref[...] = x_ref[...].astype(BF16)

    o_ref[...] = jnp.dot(xb_ref[...], w_ref[...], preferred_element_type=F32)


def _inproj(x, w):
    T, D = x.shape
    N = w.shape[1]
    tm = _pick(T, (1024, 512, 256, 128))
    tn = _pick(N, (512, 256, 128))
    return pl.pallas_call(
        _inproj_kernel,
        grid=(T // tm, N // tn),
        in_specs=[pl.BlockSpec((tm, D), lambda i, j: (i, 0)),
                  pl.BlockSpec((D, tn), lambda i, j: (0, j))],
        out_specs=pl.BlockSpec((tm, tn), lambda i, j: (i, j)),
        out_shape=jax.ShapeDtypeStruct((T, N), F32),
        scratch_shapes=[pltpu.VMEM((tm, D), BF16)],
        compiler_params=_cparams(("arbitrary", "arbitrary")),
        name="inproj",
    )(x, w)


def _local_mixers_kernel(h_ref, bg_ref, cg_ref, su_ref, sv_ref, pz_ref,
                         convw_ref, lng_ref, lnb_ref, sguw_ref, sgubt_ref, poolw_ref, pools_ref,
                         yconv_ref, ysgu_ref, ypool_ref,
                         zbuf_ref, pbuf_ref, *, tb):
    i = pl.program_id(0)

    @pl.when(i == 0)
    def _():
        zbuf_ref[0:CONV_HALO, :] = jnp.zeros((CONV_HALO, zbuf_ref.shape[1]), F32)
        pbuf_ref[0:POOL_HALO, :] = jnp.zeros((POOL_HALO, pbuf_ref.shape[1]), F32)

    zbuf_ref[CONV_HALO:CONV_HALO + tb, :] = cg_ref[...] * h_ref[...]
    conv = zbuf_ref[CONV_HALO - (CONV_W - 1):CONV_HALO - (CONV_W - 1) + tb, :] * convw_ref[0:1, :]
    for j in range(1, CONV_W):
        off = CONV_HALO - (CONV_W - 1 - j)
        conv = conv + zbuf_ref[off:off + tb, :] * convw_ref[j:j + 1, :]
    yconv_ref[...] = (bg_ref[...] * conv).astype(yconv_ref.dtype)
    zbuf_ref[0:CONV_HALO, :] = zbuf_ref[tb:tb + CONV_HALO, :]

    u = _gelu_tanh(su_ref[...])
    v = _layer_norm(_gelu_tanh(sv_ref[...]), lng_ref[...], lnb_ref[...], LN_EPS).astype(BF16)
    n_heads = sguw_ref.shape[0]
    row = lax.broadcasted_iota(jnp.int32, (SGU_CHUNK, SGU_CHUNK), 0)
    col = lax.broadcasted_iota(jnp.int32, (SGU_CHUNK, SGU_CHUNK), 1)
    causal = col <= row
    for hd in range(n_heads):
        w_h = jnp.where(causal, sguw_ref[hd], 0.0).astype(BF16)
        bias = sgubt_ref[:, hd:hd + 1]
        cs = slice(hd * SGU_HEAD, (hd + 1) * SGU_HEAD)
        for c in range(tb // SGU_CHUNK):
            rs = slice(c * SGU_CHUNK, (c + 1) * SGU_CHUNK)
            s = jnp.dot(w_h, v[rs, cs], preferred_element_type=F32) + bias
            ysgu_ref[rs, cs] = (u[rs, cs] * s).astype(ysgu_ref.dtype)

    pbuf_ref[POOL_HALO:POOL_HALO + tb, :] = pz_ref[...]
    pg = poolw_ref.shape[1]
    t_glob = i * tb + lax.broadcasted_iota(jnp.int32, (tb, 1), 0)
    for gi, win in enumerate(POOL_WINDOWS):
        cs = slice(gi * pg, (gi + 1) * pg)
        z = pbuf_ref[POOL_HALO:POOL_HALO + tb, cs]
        acc = z
        for j in range(1, win):
            acc = acc + pbuf_ref[POOL_HALO - j:POOL_HALO - j + tb, cs]
        cnt = jnp.minimum(t_glob + 1, win).astype(F32)
        d = acc / cnt - z
        y = _bdot(d, poolw_ref[gi])
        ypool_ref[:, cs] = (y * pools_ref[:, cs]).astype(ypool_ref.dtype)
    pbuf_ref[0:POOL_HALO, :] = pbuf_ref[tb:tb + POOL_HALO, :]


def _local_mixers(proj, G, p):
    T = proj.shape[0]
    tb = _pick(T, (256, 128))
    n_sgu = G // SGU_HEAD
    col = lambda c: pl.BlockSpec((tb, G), lambda i, c=c: (i, c))
    full = lambda a: pl.BlockSpec(a.shape, lambda i, n=a.ndim: (0,) * n)
    convw = p['conv_w']
    lng = p['sgu_ln_g'].reshape(1, G)
    lnb = p['sgu_ln_b'].reshape(1, G)
    sguw = p['sgu_w']
    sgubt = p['sgu_b'].T
    poolw = p['pool_w'].astype(BF16)
    pools = p['pool_scale'].reshape(1, G)
    outs = pl.pallas_call(
        functools.partial(_local_mixers_kernel, tb=tb),
        grid=(T // tb,),
        in_specs=[col(0), col(1), col(2), col(6), col(7), col(8),
                  full(convw), full(lng), full(lnb), full(sguw), full(sgubt), full(poolw), full(pools)],
        out_specs=[pl.BlockSpec((tb, G), lambda i: (i, 0))] * 3,
        out_shape=[jax.ShapeDtypeStruct((T, G), BF16)] * 3,
        scratch_shapes=[pltpu.VMEM((tb + CONV_HALO, G), F32), pltpu.VMEM((tb + POOL_HALO, G), F32)],
        compiler_params=_cparams(("arbitrary",)),
        name="local_mixers",
    )(proj, proj, proj, proj, proj, proj, convw, lng, lnb, sguw, sgubt, poolw, pools)
    assert n_sgu == sguw.shape[0]
    return outs


def _each(fn, *lists):
    return [fn(*args) for args in zip(*lists)]


def _unit_lower_inverse(Ls, eye, row, col):
    blk16 = (row >> 4) == (col >> 4)
    blk32 = (row >> 5) == (col >> 5)
    P = [jnp.where(blk16, L, 0.0) for L in Ls]
    T = [eye + p for p in P]
    for _ in range(3):
        P = _each(_pdot, P, P)
        T = _each(lambda t, p: t + _pdot(t, p), T, P)
    for off_diag in (blk32 & jnp.logical_not(blk16), jnp.logical_not(blk32)):
        X = _each(lambda L, t: _pdot(jnp.where(off_diag, L, 0.0), t), Ls, T)
        T = _each(lambda t, x: t + _pdot(t, x), T, X)
    return T


def _block_diag(x):
    x = x.astype(BF16)
    first = lax.broadcasted_iota(jnp.int32, x.shape, 1) < RWKV_HEAD
    zero = jnp.zeros_like(x)
    return jnp.concatenate([jnp.where(first, x, zero), jnp.where(first, zero, x)], axis=0)


def _fold_diag(x):
    first = lax.broadcasted_iota(jnp.int32, (RWKV_HEAD, LANES), 1) < RWKV_HEAD
    return jnp.where(first, x[:RWKV_HEAD], x[RWKV_HEAD:])


def _pdot(a, b):
    return jnp.dot(a.astype(BF16), _block_diag(b), preferred_element_type=F32)


def _pdot_nt(a, b):
    return lax.dot_general(a.astype(BF16), _block_diag(b), (((1,), (1,)), ((), ())), preferred_element_type=F32)


def _pdot_tn(a, b):
    return _fold_diag(_bdot_tn(a, b))


def _bdot_nt(a, b):
    return lax.dot_general(a.astype(BF16), b.astype(BF16), (((1,), (1,)), ((), ())), preferred_element_type=F32)


def _bdot_tn(a, b):
    return lax.dot_general(a.astype(BF16), b.astype(BF16), (((0,), (0,)), ((), ())), preferred_element_type=F32)


def _wkv_chunk_operators(at, rt, bt, kt, bh, kh, v):
    C = at[0].shape[0]
    assert C == RWKV_HEAD
    row = lax.broadcasted_iota(jnp.int32, (C, LANES), 0)
    col = lax.broadcasted_iota(jnp.int32, (C, LANES), 1) & (RWKV_HEAD - 1)
    incl = col <= row
    strict = col < row
    eye = jnp.where(row == col, 1.0, 0.0).astype(F32)
    ar = _each(lambda x, y: jnp.concatenate([x, y.astype(BF16)], axis=0), at, rt)
    p_b = _each(_pdot_nt, ar, bt)
    p_k = _each(_pdot_nt, ar, kt)
    Lab = [jnp.where(strict, p[:C], 0.0) for p in p_b]
    Mak = [jnp.where(strict, p[:C], 0.0) for p in p_k]
    Mrb = [jnp.where(incl, p[C:], 0.0).astype(BF16) for p in p_b]
    Mrk = [jnp.where(incl, p[C:], 0.0) for p in p_k]
    MakV = _each(_pdot, Mak, v)
    MrkV = _each(_pdot, Mrk, v)
    KV = _each(_pdot_tn, v, kh)
    Tinv = [t.astype(BF16) for t in _unit_lower_inverse(Lab, eye, row, col)]
    W = [w.astype(BF16) for w in _each(_pdot, Tinv, at)]
    U = [u.astype(BF16) for u in _each(_pdot, Tinv, MakV)]
    Q = _each(lambda x, m, w: x + _pdot(m, w), rt, Mrb, W)
    y_add = _each(lambda m, u, mv: _pdot(m, u) + mv, Mrb, U, MrkV)
    m_state = _each(_pdot_tn, W, bh)
    s_add = _each(lambda u, b_, kv: _pdot_tn(u, b_) + kv, U, bh, KV)
    return Q, y_add, m_state, s_add


def _rwkv_kernel(*refs, tb, has_vres):
    if has_vres:
        (r_ref, k_ref, v_ref, sm_ref, vfirst_ref,
         mur_ref, muk_ref, muv_ref, mus_ref,
         w0_ref, wup_ref, a0_ref, aup_ref, gup_ref, v0_ref, vup_ref,
         kk_ref, ka_ref, rk_ref, lng_ref, lnb_ref,
         y_ref,
         prev_ref, prevs_ref, S_ref, ybuf_ref) = refs
    else:
        (r_ref, k_ref, v_ref, sm_ref,
         mur_ref, muk_ref, muv_ref, mus_ref,
         w0_ref, wup_ref, a0_ref, aup_ref, gup_ref,
         kk_ref, ka_ref, rk_ref, lng_ref, lnb_ref,
         y_ref, vfirst_out_ref,
         prev_ref, prevs_ref, S_ref, ybuf_ref) = refs
    t = pl.program_id(1)

    @pl.when(t == 0)
    def _():
        prev_ref[...] = jnp.zeros(prev_ref.shape, F32)
        prevs_ref[...] = jnp.zeros(prevs_ref.shape, F32)
        S_ref[...] = jnp.zeros(S_ref.shape, F32)

    def shift_mix(raw, prev_row, mu):
        first = lax.broadcasted_iota(jnp.int32, raw.shape, 0) == 0
        sh = jnp.where(first, prev_row, pltpu.roll(raw, 1, 0))
        return raw + (sh - raw) * mu

    r_raw, k_raw, v_raw, sm_raw = r_ref[...], k_ref[...], v_ref[...], sm_ref[...]
    r = shift_mix(r_raw, prev_ref[0, 7:8, :], mur_ref[...])
    k = shift_mix(k_raw, prev_ref[1, 7:8, :], muk_ref[...])
    v = shift_mix(v_raw, prev_ref[2, 7:8, :], muv_ref[...])
    sm = shift_mix(sm_raw, prevs_ref[7:8, :], mus_ref[...])
    prev_ref[0] = r_raw[tb - 8:tb, :]
    prev_ref[1] = k_raw[tb - 8:tb, :]
    prev_ref[2] = v_raw[tb - 8:tb, :]
    prevs_ref[...] = sm_raw[tb - 8:tb, :]

    dl, al, gl = wup_ref.shape[0], aup_ref.shape[0], gup_ref.shape[0]
    wd = sm[:, 0:dl]
    ad = sm[:, dl:dl + al]
    gd = sm[:, dl + al:dl + al + gl]
    w = -_softplus(-(w0_ref[...] + _bdot(jnp.tanh(wd), wup_ref[...]))) - 0.5
    lw = -jnp.exp(w)
    a = _sigmoid(a0_ref[...] + _bdot(ad, aup_ref[...]))
    g = _bdot(_sigmoid(gd), gup_ref[...])
    if has_vres:
        ml = vup_ref.shape[0]
        vd = sm[:, dl + al + gl:dl + al + gl + ml]
        v = v + (vfirst_ref[...] - v) * _sigmoid(v0_ref[...] + _bdot(vd, vup_ref[...]))
    else:
        vfirst_out_ref[...] = v

    li = lax.broadcasted_iota(jnp.int32, (LANES, LANES), 0) // RWKV_HEAD
    lj = lax.broadcasted_iota(jnp.int32, (LANES, LANES), 1) // RWKV_HEAD
    head_ones = jnp.where(li == lj, 1.0, 0.0).astype(BF16)
    width = y_ref.shape[1]
    n_pairs = width // LANES

    def split3(z):
        hi = z.astype(BF16)
        r1 = z - hi.astype(F32)
        mid = r1.astype(BF16)
        return hi, mid, (r1 - mid.astype(F32)).astype(BF16)

    def head_sum(z):
        parts = split3(z)
        return jnp.concatenate(
            [sum(jnp.dot(q[:, j * LANES:(j + 1) * LANES], head_ones, preferred_element_type=F32) for q in parts)
             for j in range(n_pairs)], axis=1)

    kk = k * kk_ref[...]
    kk = kk / jnp.maximum(jnp.sqrt(head_sum(kk * kk)), 1e-12)
    k = k * (1.0 + (a - 1.0) * ka_ref[...])

    C = WKV_CHUNK
    n_chunks = tb // C
    ti = lax.broadcasted_iota(jnp.int32, (tb, tb), 0)
    tj = lax.broadcasted_iota(jnp.int32, (tb, tb), 1)
    tri = jnp.where((tj <= ti) & ((ti // C) == (tj // C)), 1.0, 0.0).astype(BF16)
    lc = sum(jnp.dot(tri, q, preferred_element_type=F32) for q in split3(lw))
    lc_end = jnp.concatenate(
        [jnp.broadcast_to(lc[(c + 1) * C - 1:(c + 1) * C, :], (C, width)) for c in range(n_chunks)], axis=0)
    e_neg = jnp.exp(-lc)
    e_end = jnp.exp(lc_end - lc)
    at = (-kk * jnp.exp(lc - lw)).astype(BF16)
    rt = r * jnp.exp(lc)
    bt = (kk * a * e_neg).astype(BF16)
    kt = (k * e_neg).astype(BF16)
    bh = (kk * a * e_end).astype(BF16)
    kh = (k * e_end).astype(BF16)
    vb = v.astype(BF16)
    decay_end = jnp.exp(lc_end)

    tiles = [(c, j) for c in range(n_chunks) for j in range(n_pairs)]
    cut = lambda z: [z[c * C:(c + 1) * C, j * LANES:(j + 1) * LANES] for c, j in tiles]
    Q, y_add, m_state, s_add = _wkv_chunk_operators(*[cut(z) for z in (at, rt, bt, kt, bh, kh, vb)])
    S = [S_ref[j] for j in range(n_pairs)]
    for c in range(n_chunks):
        for j in range(n_pairs):
            i = c * n_pairs + j
            ls = slice(j * LANES, (j + 1) * LANES)
            ybuf_ref[c * C:(c + 1) * C, ls] = _pdot_nt(Q[i], S[j]) + y_add[i]
            S[j] = S[j] * decay_end[c * C:c * C + 1, ls] + _pdot(S[j], m_state[i]) + s_add[i]
    for j in range(n_pairs):
        S_ref[j] = S[j]

    y = ybuf_ref[...]
    inv_n = 1.0 / RWKV_HEAD
    mu = head_sum(y) * inv_n
    yc = y - mu
    var = head_sum(yc * yc) * inv_n
    y = yc * lax.rsqrt(var + RWKV_GN_EPS) * lng_ref[...] + lnb_ref[...]
    y = y + head_sum(r * k * rk_ref[...]) * v
    y_ref[...] = (y * g).astype(y_ref.dtype)


def _rwkv_mixer(proj, G, small_w, small_off, p, v_first):
    T = proj.shape[0]
    has_vres = v_first is not None
    tb = _pick(T, (256, 128, 64))
    bw = _pick(G, (RWKV_BLOCK_LANES, 2 * LANES, LANES))
    cb = G // bw
    dl, al, gl = p['decay_up'].shape[0], p['iclr_up'].shape[0], p['gate_up'].shape[0]
    ml = p['vres_up'].shape[0] if has_vres else 0
    mu = p['shift_mu']
    mu_r, mu_k, mu_v = (mu[q * G:(q + 1) * G].reshape(1, G) for q in range(3))
    mu_s = jnp.pad(mu[3 * G:], (0, small_w - (dl + al + gl + ml))).reshape(1, small_w)

    def colblk(first):
        return pl.BlockSpec((tb, bw), lambda q, t, f=first: (t, f + q))

    def vec(arr):
        return arr.reshape(1, G), pl.BlockSpec((1, bw), lambda q, t: (0, q))

    def up(arr):
        return arr.astype(BF16), pl.BlockSpec((arr.shape[0], bw), lambda q, t: (0, q))

    small_spec = pl.BlockSpec((tb, small_w), lambda q, t: (t, small_off // small_w))
    pair_spec = pl.BlockSpec((tb, bw), lambda q, t: (t, q))
    mu_spec = pl.BlockSpec((1, bw), lambda q, t: (0, q))
    args = [proj, proj, proj, proj]
    specs = [colblk(3 * cb), colblk(4 * cb), colblk(5 * cb), small_spec]
    if has_vres:
        args.append(v_first)
        specs.append(pair_spec)
    args += [mu_r, mu_k, mu_v, mu_s]
    specs += [mu_spec, mu_spec, mu_spec, pl.BlockSpec((1, small_w), lambda q, t: (0, 0))]
    names = ['decay_w0', 'decay_up', 'iclr_a0', 'iclr_up', 'gate_up']
    if has_vres:
        names += ['vres_v0', 'vres_up']
    names += ['k_k', 'k_a', 'r_k', 'lnx_g', 'lnx_b']
    for nm in names:
        arr, spec = up(p[nm]) if nm.endswith('_up') else vec(p[nm])
        args.append(arr)
        specs.append(spec)
    out_shape = [jax.ShapeDtypeStruct((T, G), BF16)]
    out_specs = [pair_spec]
    if not has_vres:
        out_shape.append(jax.ShapeDtypeStruct((T, G), F32))
        out_specs.append(pair_spec)
    outs = pl.pallas_call(
        functools.partial(_rwkv_kernel, tb=tb, has_vres=has_vres),
        grid=(G // bw, T // tb),
        in_specs=specs,
        out_specs=out_specs,
        out_shape=out_shape,
        scratch_shapes=[pltpu.VMEM((3, 8, bw), F32), pltpu.VMEM((8, small_w), F32),
                        pltpu.VMEM((bw // LANES, RWKV_HEAD, LANES), F32),
                        pltpu.VMEM((tb, bw), F32)],
        compiler_params=_cparams(("arbitrary", "arbitrary")),
        name="rwkv7_mixer",
    )(*args)
    if has_vres:
        return outs[0], v_first
    return outs[0], outs[1]


def _outproj_kernel(y0_ref, y1_ref, y2_ref, y3_ref, w_ref, x_ref, g_ref, b_ref, o_ref, ycat_ref, *, tn):
    G = y0_ref.shape[1]
    for j, y_ref in enumerate((y0_ref, y1_ref, y2_ref, y3_ref)):
        ycat_ref[:, j * G:(j + 1) * G] = y_ref[...]
    ycat = ycat_ref[...]
    for n in range(0, o_ref.shape[1], tn):
        o_ref[:, n:n + tn] = (DEEPNORM_ALPHA * x_ref[:, n:n + tn]
                              + jnp.dot(ycat, w_ref[:, n:n + tn], preferred_element_type=F32))
    o_ref[...] = _layer_norm(o_ref[...], g_ref[...], b_ref[...], LN_EPS)


def _outproj_ln(ys, w, x, g, b):
    T, D = x.shape
    G = ys[0].shape[1]
    tm = _pick(T, (256, 128))
    yspec = pl.BlockSpec((tm, G), lambda i: (i, 0))
    row = pl.BlockSpec((tm, D), lambda i: (i, 0))
    vec = pl.BlockSpec((1, D), lambda i: (0, 0))
    wspec = pl.BlockSpec(w.shape, lambda i: (0, 0), pipeline_mode=pl.Buffered(1))
    return pl.pallas_call(
        functools.partial(_outproj_kernel, tn=_pick(D, (512, 256, 128))),
        grid=(T // tm,),
        in_specs=[yspec] * 4 + [wspec, row, vec, vec],
        out_specs=row,
        out_shape=jax.ShapeDtypeStruct((T, D), F32),
        scratch_shapes=[pltpu.VMEM((tm, len(ys) * G), BF16)],
        compiler_params=_cparams(("arbitrary",)),
        name="outproj_ln",
    )(*ys, w, x, g.reshape(1, D), b.reshape(1, D))


def _ffn_kernel(x_ref, wg_ref, wu_ref, wd_ref, g_ref, b_ref, o_ref, xb_ref, *, tn):
    f = pl.program_id(1)
    nf = pl.num_programs(1)

    tm = o_ref.shape[0]
    tr = _pick(tm, (256, 128))

    def rows(c):
        return pl.ds(pl.multiple_of(c * tr, tr), tr)

    @pl.when(f == 0)
    def _():
        def body(c, carry):
            xr = x_ref[rows(c), :]
            xb_ref[rows(c), :] = xr.astype(BF16)
            o_ref[rows(c), :] = DEEPNORM_ALPHA * xr
            return carry
        lax.fori_loop(0, tm // tr, body, 0)

    xb = xb_ref[...]
    gate = jnp.dot(xb, wg_ref[...], preferred_element_type=F32)
    upv = jnp.dot(xb, wu_ref[...], preferred_element_type=F32)
    hid = (gate * _sigmoid(gate) * upv).astype(BF16)
    for n in range(0, o_ref.shape[1], tn):
        o_ref[:, n:n + tn] += jnp.dot(hid, wd_ref[:, n:n + tn], preferred_element_type=F32)

    @pl.when(f == nf - 1)
    def _():
        def body(c, carry):
            o_ref[rows(c), :] = _layer_norm(o_ref[rows(c), :], g_ref[...], b_ref[...], LN_EPS)
            return carry
        lax.fori_loop(0, tm // tr, body, 0)


def _ffn_ln(x, wg, wu, wd, g, b):
    T, D = x.shape
    F = wg.shape[1]
    tm = _pick(T, (1024, 512, 256, 128))
    tf = _pick(F, (256, 128))
    row = pl.BlockSpec((tm, D), lambda i, f: (i, 0), pipeline_mode=pl.Buffered(1))
    row_once = row
    vec = pl.BlockSpec((1, D), lambda i, f: (0, 0))
    return pl.pallas_call(
        functools.partial(_ffn_kernel, tn=_pick(D, (512, 256, 128))),
        grid=(T // tm, F // tf),
        in_specs=[row_once, pl.BlockSpec((D, tf), lambda i, f: (0, f)), pl.BlockSpec((D, tf), lambda i, f: (0, f)),
                  pl.BlockSpec((tf, D), lambda i, f: (f, 0)), vec, vec],
        out_specs=row,
        out_shape=jax.ShapeDtypeStruct((T, D), F32),
        scratch_shapes=[pltpu.VMEM((tm, D), BF16)],
        compiler_params=_cparams(("arbitrary", "arbitrary")),
        name="ffn_ln",
    )(x, wg, wu, wd, g.reshape(1, D), b.reshape(1, D))


def _hybrid_layer(x, v_first, p):
    T, D = x.shape
    G = D // 4
    w_in = p['w_in']
    n_small = w_in.shape[1] - 9 * G
    small_w = _round_up(n_small, 2 * LANES)
    assert (9 * G) % small_w == 0
    w_perm = jnp.concatenate(
        [w_in[:, :6 * G], w_in[:, 6 * G + n_small:], w_in[:, 6 * G:6 * G + n_small],
         jnp.zeros((D, small_w - n_small), w_in.dtype)], axis=1).astype(BF16)
    proj = _inproj(x, w_perm)
    y_conv, y_sgu, y_pool = _local_mixers(proj, G, p)
    y_rwkv, v_first = _rwkv_mixer(proj, G, small_w, 9 * G, p, v_first)
    x = _outproj_ln((y_conv, y_rwkv, y_sgu, y_pool), p['w_out'].astype(BF16), x, p['ln_mix_g'], p['ln_mix_b'])
    x = _ffn_ln(x, p['ffn_gate'].astype(BF16), p['ffn_up'].astype(BF16), p['ffn_down'].astype(BF16),
                p['ln_ffn_g'], p['ln_ffn_b'])
    return x, v_first


_NAMES_0 = ('w_in', 'conv_w', 'shift_mu', 'decay_w0', 'decay_up', 'iclr_a0', 'iclr_up', 'gate_up',
            'k_k', 'k_a', 'r_k', 'lnx_g', 'lnx_b', 'sgu_ln_g', 'sgu_ln_b', 'sgu_w', 'sgu_b',
            'pool_w', 'pool_scale', 'w_out', 'ln_mix_g', 'ln_mix_b', 'ffn_gate', 'ffn_up', 'ffn_down',
            'ln_ffn_g', 'ln_ffn_b')
_NAMES_1 = _NAMES_0[:7] + ('vres_v0', 'vres_up') + _NAMES_0[7:]


def kernel(x, w_in_0, conv_w_0, shift_mu_0, decay_w0_0, decay_up_0, iclr_a0_0, iclr_up_0, gate_up_0, k_k_0, k_a_0, r_k_0, lnx_g_0, lnx_b_0, sgu_ln_g_0, sgu_ln_b_0, sgu_w_0, sgu_b_0, pool_w_0, pool_scale_0, w_out_0, ln_mix_g_0, ln_mix_b_0, ffn_gate_0, ffn_up_0, ffn_down_0, ln_ffn_g_0, ln_ffn_b_0, w_in_1, conv_w_1, shift_mu_1, decay_w0_1, decay_up_1, iclr_a0_1, iclr_up_1, vres_v0_1, vres_up_1, gate_up_1, k_k_1, k_a_1, r_k_1, lnx_g_1, lnx_b_1, sgu_ln_g_1, sgu_ln_b_1, sgu_w_1, sgu_b_1, pool_w_1, pool_scale_1, w_out_1, ln_mix_g_1, ln_mix_b_1, ffn_gate_1, ffn_up_1, ffn_down_1, ln_ffn_g_1, ln_ffn_b_1):
    p0 = dict(zip(_NAMES_0, (w_in_0, conv_w_0, shift_mu_0, decay_w0_0, decay_up_0, iclr_a0_0, iclr_up_0, gate_up_0, k_k_0, k_a_0, r_k_0, lnx_g_0, lnx_b_0, sgu_ln_g_0, sgu_ln_b_0, sgu_w_0, sgu_b_0, pool_w_0, pool_scale_0, w_out_0, ln_mix_g_0, ln_mix_b_0, ffn_gate_0, ffn_up_0, ffn_down_0, ln_ffn_g_0, ln_ffn_b_0)))
    p1 = dict(zip(_NAMES_1, (w_in_1, conv_w_1, shift_mu_1, decay_w0_1, decay_up_1, iclr_a0_1, iclr_up_1, vres_v0_1, vres_up_1, gate_up_1, k_k_1, k_a_1, r_k_1, lnx_g_1, lnx_b_1, sgu_ln_g_1, sgu_ln_b_1, sgu_w_1, sgu_b_1, pool_w_1, pool_scale_1, w_out_1, ln_mix_g_1, ln_mix_b_1, ffn_gate_1, ffn_up_1, ffn_down_1, ln_ffn_g_1, ln_ffn_b_1)))
    B, T, D = x.shape
    assert B == 1
    h = x.reshape(T, D)
    h, v_first = _hybrid_layer(h, None, p0)
    h, _ = _hybrid_layer(h, v_first, p1)
    return h.reshape(B, T, D)
```

```python
import functools

import jax
import jax.numpy as jnp
from jax import lax
from jax.experimental import pallas as pl
from jax.experimental.pallas import tpu as pltpu

LANES = 128
RWKV_HEAD = 64
SGU_CHUNK = 128
SGU_HEAD = 128
CONV_W = 3
POOL_WINDOWS = (2, 4, 8, 16)
POOL_HALO = 16
CONV_HALO = 8
LN_EPS = 1e-5
RWKV_GN_EPS = 64e-5
WKV_CHUNK = 64
RWKV_BLOCK_LANES = 1024
DEPTH = 2
DEEPNORM_ALPHA = (2 * DEPTH) ** 0.25
VMEM_LIMIT_BYTES = 60 * 1024 * 1024

F32 = jnp.float32
BF16 = jnp.bfloat16


def _cparams(sem):
    return pltpu.CompilerParams(dimension_semantics=sem, vmem_limit_bytes=VMEM_LIMIT_BYTES)


def _pick(n, prefs):
    for p in prefs:
        if n % p == 0:
            return p
    return n


def _round_up(n, m):
    return (n + m - 1) // m * m


def _bdot(a, b):
    return jnp.dot(a.astype(BF16), b.astype(BF16), preferred_element_type=F32)


def _layer_norm(x, g, b, eps):
    mu = jnp.mean(x, axis=-1, keepdims=True)
    xc = x - mu
    var = jnp.mean(xc * xc, axis=-1, keepdims=True)
    return xc * lax.rsqrt(var + eps) * g + b


def _sigmoid(x):
    return 1.0 / (1.0 + jnp.exp(-x))


def _gelu_tanh(x):
    c = 0.7978845608028654
    return 0.5 * x * (1.0 + jnp.tanh(c * (x + 0.044715 * (x * x * x))))


def _softplus(x):
    return jnp.maximum(x, 0.0) + jnp.log(1.0 + jnp.exp(-jnp.abs(x)))


def _inproj_kernel(x_ref, wa_ref, wb_ref, o_ref, xb_ref, *, n_a):
    j = pl.program_id(1)

    @pl.when(j == 0)
    def _():
        xb_ref[...] = x_ref[...].astype(BF16)

    @pl.when(j < n_a)
    def _():
        o_ref[...] = jnp.dot(xb_ref[...], wa_ref[...], preferred_element_type=F32)

    @pl.when(j >= n_a)
    def _():
        o_ref[...] = jnp.dot(xb_ref[...], wb_ref[...], preferred_element_type=F32)


def _inproj(x, w_a, w_b):
    T, D = x.shape
    Na, Nb = w_a.shape[1], w_b.shape[1]
    tm = _pick(T, (1024, 512, 256, 128))
    tn = next(c for c in (512, 256, 128) if Na % c == 0 and Nb % c == 0)
    n_a = Na // tn
    return pl.pallas_call(
        functools.partial(_inproj_kernel, n_a=n_a),
        grid=(T // tm, (Na + Nb) // tn),
        in_specs=[pl.BlockSpec((tm, D), lambda i, j: (i, 0)),
                  pl.BlockSpec((D, tn), lambda i, j: (0, jnp.minimum(j, n_a - 1))),
                  pl.BlockSpec((D, tn), lambda i, j: (0, jnp.maximum(j - n_a, 0)))],
        out_specs=pl.BlockSpec((tm, tn), lambda i, j: (i, j)),
        out_shape=jax.ShapeDtypeStruct((T, Na + Nb), F32),
        scratch_shapes=[pltpu.VMEM((tm, D), BF16)],
        compiler_params=_cparams(("arbitrary", "arbitrary")),
        name="inproj",
    )(x, w_a, w_b)


def _local_mixers_kernel(h_ref, bg_ref, cg_ref, su_ref, sv_ref, pz_ref,
                         convw_ref, lng_ref, lnb_ref, sguw_ref, sgubt_ref, poolw_ref, pools_ref,
                         yconv_ref, ysgu_ref, ypool_ref,
                         zbuf_ref, pbuf_ref, *, tb):
    i = pl.program_id(0)

    @pl.when(i == 0)
    def _():
        zbuf_ref[0:CONV_HALO, :] = jnp.zeros((CONV_HALO, zbuf_ref.shape[1]), F32)
        pbuf_ref[0:POOL_HALO, :] = jnp.zeros((POOL_HALO, pbuf_ref.shape[1]), F32)

    zbuf_ref[CONV_HALO:CONV_HALO + tb, :] = cg_ref[...] * h_ref[...]
    conv = zbuf_ref[CONV_HALO - (CONV_W - 1):CONV_HALO - (CONV_W - 1) + tb, :] * convw_ref[0:1, :]
    for j in range(1, CONV_W):
        off = CONV_HALO - (CONV_W - 1 - j)
        conv = conv + zbuf_ref[off:off + tb, :] * convw_ref[j:j + 1, :]
    yconv_ref[...] = (bg_ref[...] * conv).astype(yconv_ref.dtype)
    zbuf_ref[0:CONV_HALO, :] = zbuf_ref[tb:tb + CONV_HALO, :]

    u = _gelu_tanh(su_ref[...])
    v = _layer_norm(_gelu_tanh(sv_ref[...]), lng_ref[...], lnb_ref[...], LN_EPS).astype(BF16)
    n_heads = sguw_ref.shape[0]
    row = lax.broadcasted_iota(jnp.int32, (SGU_CHUNK, SGU_CHUNK), 0)
    col = lax.broadcasted_iota(jnp.int32, (SGU_CHUNK, SGU_CHUNK), 1)
    causal = col <= row
    for hd in range(n_heads):
        w_h = jnp.where(causal, sguw_ref[hd], 0.0).astype(BF16)
        bias = sgubt_ref[:, hd:hd + 1]
        cs = slice(hd * SGU_HEAD, (hd + 1) * SGU_HEAD)
        for c in range(tb // SGU_CHUNK):
            rs = slice(c * SGU_CHUNK, (c + 1) * SGU_CHUNK)
            s = jnp.dot(w_h, v[rs, cs], preferred_element_type=F32) + bias
            ysgu_ref[rs, cs] = (u[rs, cs] * s).astype(ysgu_ref.dtype)

    pbuf_ref[POOL_HALO:POOL_HALO + tb, :] = pz_ref[...]
    pg = poolw_ref.shape[1]
    t_glob = i * tb + lax.broadcasted_iota(jnp.int32, (tb, 1), 0)
    for gi, win in enumerate(POOL_WINDOWS):
        cs = slice(gi * pg, (gi + 1) * pg)
        z = pbuf_ref[POOL_HALO:POOL_HALO + tb, cs]
        acc = z
        for j in range(1, win):
            acc = acc + pbuf_ref[POOL_HALO - j:POOL_HALO - j + tb, cs]
        cnt = jnp.minimum(t_glob + 1, win).astype(F32)
        d = acc / cnt - z
        y = _bdot(d, poolw_ref[gi])
        ypool_ref[:, cs] = (y * pools_ref[:, cs]).astype(ypool_ref.dtype)
    pbuf_ref[0:POOL_HALO, :] = pbuf_ref[tb:tb + POOL_HALO, :]


def _local_mixers(proj, G, p):
    T = proj.shape[0]
    tb = _pick(T, (256, 128))
    n_sgu = G // SGU_HEAD
    col = lambda c: pl.BlockSpec((tb, G), lambda i, c=c: (i, c))
    full = lambda a: pl.BlockSpec(a.shape, lambda i, n=a.ndim: (0,) * n)
    convw = p['conv_w']
    lng = p['sgu_ln_g'].reshape(1, G)
    lnb = p['sgu_ln_b'].reshape(1, G)
    sguw = p['sgu_w']
    sgubt = p['sgu_b'].T
    poolw = p['pool_w'].astype(BF16)
    pools = p['pool_scale'].reshape(1, G)
    outs = pl.pallas_call(
        functools.partial(_local_mixers_kernel, tb=tb),
        grid=(T // tb,),
        in_specs=[col(0), col(1), col(2), col(6), col(7), col(8),
                  full(convw), full(lng), full(lnb), full(sguw), full(sgubt), full(poolw), full(pools)],
        out_specs=[pl.BlockSpec((tb, G), lambda i: (i, 0))] * 3,
        out_shape=[jax.ShapeDtypeStruct((T, G), BF16)] * 3,
        scratch_shapes=[pltpu.VMEM((tb + CONV_HALO, G), F32), pltpu.VMEM((tb + POOL_HALO, G), F32)],
        compiler_params=_cparams(("arbitrary",)),
        name="local_mixers",
    )(proj, proj, proj, proj, proj, proj, convw, lng, lnb, sguw, sgubt, poolw, pools)
    assert n_sgu == sguw.shape[0]
    return outs


def _each(fn, *lists):
    return [fn(*args) for args in zip(*lists)]


def _unit_lower_inverse(Ls, eye, row, col):
    blk16 = (row >> 4) == (col >> 4)
    blk32 = (row >> 5) == (col >> 5)
    P = [jnp.where(blk16, L, 0.0) for L in Ls]
    T = [eye + p for p in P]
    for _ in range(3):
        P = _each(_pdot, P, P)
        T = _each(lambda t, p: t + _pdot(t, p), T, P)
    for off_diag in (blk32 & jnp.logical_not(blk16), jnp.logical_not(blk32)):
        X = _each(lambda L, t: _pdot(jnp.where(off_diag, L, 0.0), t), Ls, T)
        T = _each(lambda t, x: t + _pdot(t, x), T, X)
    return T


def _block_diag(x):
    x = x.astype(BF16)
    first = lax.broadcasted_iota(jnp.int32, x.shape, 1) < RWKV_HEAD
    zero = jnp.zeros_like(x)
    return jnp.concatenate([jnp.where(first, x, zero), jnp.where(first, zero, x)], axis=0)


def _fold_diag(x):
    first = lax.broadcasted_iota(jnp.int32, (RWKV_HEAD, LANES), 1) < RWKV_HEAD
    return jnp.where(first, x[:RWKV_HEAD], x[RWKV_HEAD:])


def _pdot(a, b):
    return jnp.dot(a.astype(BF16), _block_diag(b), preferred_element_type=F32)


def _pdot_nt(a, b):
    return lax.dot_general(a.astype(BF16), _block_diag(b), (((1,), (1,)), ((), ())), preferred_element_type=F32)


def _pdot_tn(a, b):
    return _fold_diag(_bdot_tn(a, b))


def _bdot_tn(a, b):
    return lax.dot_general(a.astype(BF16), b.astype(BF16), (((0,), (0,)), ((), ())), preferred_element_type=F32)


def _wkv_chunk_operators(at, rt, bt, kt, bh, kh, v):
    C = at[0].shape[0]
    assert C == RWKV_HEAD
    row = lax.broadcasted_iota(jnp.int32, (C, LANES), 0)
    col = lax.broadcasted_iota(jnp.int32, (C, LANES), 1) & (RWKV_HEAD - 1)
    incl = col <= row
    strict = col < row
    eye = jnp.where(row == col, 1.0, 0.0).astype(F32)
    ar = _each(lambda x, y: jnp.concatenate([x, y.astype(BF16)], axis=0), at, rt)
    p_b = _each(_pdot_nt, ar, bt)
    p_k = _each(_pdot_nt, ar, kt)
    Lab = [jnp.where(strict, p[:C], 0.0) for p in p_b]
    Mak = [jnp.where(strict, p[:C], 0.0) for p in p_k]
    Mrb = [jnp.where(incl, p[C:], 0.0).astype(BF16) for p in p_b]
    Mrk = [jnp.where(incl, p[C:], 0.0) for p in p_k]
    MakV = _each(_pdot, Mak, v)
    MrkV = _each(_pdot, Mrk, v)
    KV = _each(_pdot_tn, v, kh)
    Tinv = [t.astype(BF16) for t in _unit_lower_inverse(Lab, eye, row, col)]
    W = [w.astype(BF16) for w in _each(_pdot, Tinv, at)]
    U = [u.astype(BF16) for u in _each(_pdot, Tinv, MakV)]
    Q = _each(lambda x, m, w: x + _pdot(m, w), rt, Mrb, W)
    y_add = _each(lambda m, u, mv: _pdot(m, u) + mv, Mrb, U, MrkV)
    m_state = _each(_pdot_tn, W, bh)
    s_add = _each(lambda u, b_, kv: _pdot_tn(u, b_) + kv, U, bh, KV)
    return Q, y_add, m_state, s_add


def _rwkv_kernel(*refs, tb, has_vres):
    if has_vres:
        (r_ref, k_ref, v_ref, sm_ref, vfirst_ref,
         mur_ref, muk_ref, muv_ref, mus_ref,
         w0_ref, wup_ref, a0_ref, aup_ref, gup_ref, v0_ref, vup_ref,
         kk_ref, ka_ref, rk_ref, lng_ref, lnb_ref,
         y_ref,
         prev_ref, prevs_ref, S_ref, ybuf_ref) = refs
    else:
        (r_ref, k_ref, v_ref, sm_ref,
         mur_ref, muk_ref, muv_ref, mus_ref,
         w0_ref, wup_ref, a0_ref, aup_ref, gup_ref,
         kk_ref, ka_ref, rk_ref, lng_ref, lnb_ref,
         y_ref, vfirst_out_ref,
         prev_ref, prevs_ref, S_ref, ybuf_ref) = refs
    t = pl.program_id(1)

    @pl.when(t == 0)
    def _():
        prev_ref[...] = jnp.zeros(prev_ref.shape, F32)
        prevs_ref[...] = jnp.zeros(prevs_ref.shape, F32)
        S_ref[...] = jnp.zeros(S_ref.shape, F32)

    def shift_mix(raw, prev_row, mu):
        first = lax.broadcasted_iota(jnp.int32, raw.shape, 0) == 0
        sh = jnp.where(first, prev_row, pltpu.roll(raw, 1, 0))
        return raw + (sh - raw) * mu

    r_raw, k_raw, v_raw, sm_raw = r_ref[...], k_ref[...], v_ref[...], sm_ref[...]
    r = shift_mix(r_raw, prev_ref[0, 7:8, :], mur_ref[...])
    k = shift_mix(k_raw, prev_ref[1, 7:8, :], muk_ref[...])
    v = shift_mix(v_raw, prev_ref[2, 7:8, :], muv_ref[...])
    sm = shift_mix(sm_raw, prevs_ref[7:8, :], mus_ref[...])
    prev_ref[0] = r_raw[tb - 8:tb, :]
    prev_ref[1] = k_raw[tb - 8:tb, :]
    prev_ref[2] = v_raw[tb - 8:tb, :]
    prevs_ref[...] = sm_raw[tb - 8:tb, :]

    dl, al, gl = wup_ref.shape[0], aup_ref.shape[0], gup_ref.shape[0]
    wd = sm[:, 0:dl]
    ad = sm[:, dl:dl + al]
    gd = sm[:, dl + al:dl + al + gl]
    w = -_softplus(-(w0_ref[...] + _bdot(jnp.tanh(wd), wup_ref[...]))) - 0.5
    lw = -jnp.exp(w)
    a = _sigmoid(a0_ref[...] + _bdot(ad, aup_ref[...]))
    g = _bdot(_sigmoid(gd), gup_ref[...])
    if has_vres:
        ml = vup_ref.shape[0]
        vd = sm[:, dl + al + gl:dl + al + gl + ml]
        v = v + (vfirst_ref[...] - v) * _sigmoid(v0_ref[...] + _bdot(vd, vup_ref[...]))
    else:
        vfirst_out_ref[...] = v

    li = lax.broadcasted_iota(jnp.int32, (LANES, LANES), 0) // RWKV_HEAD
    lj = lax.broadcasted_iota(jnp.int32, (LANES, LANES), 1) // RWKV_HEAD
    head_ones = jnp.where(li == lj, 1.0, 0.0).astype(BF16)
    width = y_ref.shape[1]
    n_pairs = width // LANES

    def split3(z):
        hi = z.astype(BF16)
        r1 = z - hi.astype(F32)
        mid = r1.astype(BF16)
        return hi, mid, (r1 - mid.astype(F32)).astype(BF16)

    def head_sum(z):
        parts = split3(z)
        return jnp.concatenate(
            [sum(jnp.dot(q[:, j * LANES:(j + 1) * LANES], head_ones, preferred_element_type=F32) for q in parts)
             for j in range(n_pairs)], axis=1)

    kk = k * kk_ref[...]
    kk = kk / jnp.maximum(jnp.sqrt(head_sum(kk * kk)), 1e-12)
    k = k * (1.0 + (a - 1.0) * ka_ref[...])

    C = WKV_CHUNK
    n_chunks = tb // C
    ti = lax.broadcasted_iota(jnp.int32, (tb, tb), 0)
    tj = lax.broadcasted_iota(jnp.int32, (tb, tb), 1)
    tri = jnp.where((tj <= ti) & ((ti // C) == (tj // C)), 1.0, 0.0).astype(BF16)
    lc = sum(jnp.dot(tri, q, preferred_element_type=F32) for q in split3(lw))
    lc_end = jnp.concatenate(
        [jnp.broadcast_to(lc[(c + 1) * C - 1:(c + 1) * C, :], (C, width)) for c in range(n_chunks)], axis=0)
    e_neg = jnp.exp(-lc)
    e_end = jnp.exp(lc_end - lc)
    at = (-kk * jnp.exp(lc - lw)).astype(BF16)
    rt = r * jnp.exp(lc)
    bt = (kk * a * e_neg).astype(BF16)
    kt = (k * e_neg).astype(BF16)
    bh = (kk * a * e_end).astype(BF16)
    kh = (k * e_end).astype(BF16)
    vb = v.astype(BF16)
    decay_end = jnp.exp(lc_end)

    tiles = [(c, j) for c in range(n_chunks) for j in range(n_pairs)]
    cut = lambda z: [z[c * C:(c + 1) * C, j * LANES:(j + 1) * LANES] for c, j in tiles]
    Q, y_add, m_state, s_add = _wkv_chunk_operators(*[cut(z) for z in (at, rt, bt, kt, bh, kh, vb)])
    S = [S_ref[j] for j in range(n_pairs)]
    for c in range(n_chunks):
        for j in range(n_pairs):
            i = c * n_pairs + j
            ls = slice(j * LANES, (j + 1) * LANES)
            ybuf_ref[c * C:(c + 1) * C, ls] = _pdot_nt(Q[i], S[j]) + y_add[i]
            S[j] = S[j] * decay_end[c * C:c * C + 1, ls] + _pdot(S[j], m_state[i]) + s_add[i]
    for j in range(n_pairs):
        S_ref[j] = S[j]

    y = ybuf_ref[...]
    inv_n = 1.0 / RWKV_HEAD
    mu = head_sum(y) * inv_n
    yc = y - mu
    var = head_sum(yc * yc) * inv_n
    y = yc * lax.rsqrt(var + RWKV_GN_EPS) * lng_ref[...] + lnb_ref[...]
    y = y + head_sum(r * k * rk_ref[...]) * v
    y_ref[...] = (y * g).astype(y_ref.dtype)


def _rwkv_mixer(proj, G, small_w, small_off, p, v_first):
    T = proj.shape[0]
    has_vres = v_first is not None
    tb = _pick(T, (256, 128, 64))
    bw = _pick(G, (RWKV_BLOCK_LANES, 2 * LANES, LANES))
    cb = G // bw
    dl, al, gl = p['decay_up'].shape[0], p['iclr_up'].shape[0], p['gate_up'].shape[0]
    ml = p['vres_up'].shape[0] if has_vres else 0
    mu = p['shift_mu']
    mu_r, mu_k, mu_v = (mu[q * G:(q + 1) * G].reshape(1, G) for q in range(3))
    mu_s = jnp.pad(mu[3 * G:], (0, small_w - (dl + al + gl + ml))).reshape(1, small_w)

    def colblk(first):
        return pl.BlockSpec((tb, bw), lambda q, t, f=first: (t, f + q))

    def vec(arr):
        return arr.reshape(1, G), pl.BlockSpec((1, bw), lambda q, t: (0, q))

    def up(arr):
        return arr.astype(BF16), pl.BlockSpec((arr.shape[0], bw), lambda q, t: (0, q))

    small_spec = pl.BlockSpec((tb, small_w), lambda q, t: (t, small_off // small_w))
    pair_spec = pl.BlockSpec((tb, bw), lambda q, t: (t, q))
    mu_spec = pl.BlockSpec((1, bw), lambda q, t: (0, q))
    args = [proj, proj, proj, proj]
    specs = [colblk(3 * cb), colblk(4 * cb), colblk(5 * cb), small_spec]
    if has_vres:
        args.append(v_first)
        specs.append(pair_spec)
    args += [mu_r, mu_k, mu_v, mu_s]
    specs += [mu_spec, mu_spec, mu_spec, pl.BlockSpec((1, small_w), lambda q, t: (0, 0))]
    names = ['decay_w0', 'decay_up', 'iclr_a0', 'iclr_up', 'gate_up']
    if has_vres:
        names += ['vres_v0', 'vres_up']
    names += ['k_k', 'k_a', 'r_k', 'lnx_g', 'lnx_b']
    for nm in names:
        arr, spec = up(p[nm]) if nm.endswith('_up') else vec(p[nm])
        args.append(arr)
        specs.append(spec)
    out_shape = [jax.ShapeDtypeStruct((T, G), BF16)]
    out_specs = [pair_spec]
    if not has_vres:
        out_shape.append(jax.ShapeDtypeStruct((T, G), F32))
        out_specs.append(pair_spec)
    outs = pl.pallas_call(
        functools.partial(_rwkv_kernel, tb=tb, has_vres=has_vres),
        grid=(G // bw, T // tb),
        in_specs=specs,
        out_specs=out_specs,
        out_shape=out_shape,
        scratch_shapes=[pltpu.VMEM((3, 8, bw), F32), pltpu.VMEM((8, small_w), F32),
                        pltpu.VMEM((bw // LANES, RWKV_HEAD, LANES), F32),
                        pltpu.VMEM((tb, bw), F32)],
        compiler_params=_cparams(("arbitrary", "arbitrary")),
        name="rwkv7_mixer",
    )(*args)
    if has_vres:
        return outs[0], v_first
    return outs[0], outs[1]


def _outproj_kernel(y0_ref, y1_ref, y2_ref, y3_ref, w_ref, x_ref, g_ref, b_ref, o_ref, ycat_ref, *, tn):
    G = y0_ref.shape[1]
    for j, y_ref in enumerate((y0_ref, y1_ref, y2_ref, y3_ref)):
        ycat_ref[:, j * G:(j + 1) * G] = y_ref[...]
    ycat = ycat_ref[...]
    for n in range(0, o_ref.shape[1], tn):
        o_ref[:, n:n + tn] = (DEEPNORM_ALPHA * x_ref[:, n:n + tn]
                              + jnp.dot(ycat, w_ref[:, n:n + tn], preferred_element_type=F32))
    o_ref[...] = _layer_norm(o_ref[...], g_ref[...], b_ref[...], LN_EPS)


def _outproj_ln(ys, w, x, g, b):
    T, D = x.shape
    G = ys[0].shape[1]
    tm = _pick(T, (256, 128))
    yspec = pl.BlockSpec((tm, G), lambda i: (i, 0))
    row = pl.BlockSpec((tm, D), lambda i: (i, 0))
    vec = pl.BlockSpec((1, D), lambda i: (0, 0))
    wspec = pl.BlockSpec(w.shape, lambda i: (0, 0), pipeline_mode=pl.Buffered(1))
    return pl.pallas_call(
        functools.partial(_outproj_kernel, tn=_pick(D, (512, 256, 128))),
        grid=(T // tm,),
        in_specs=[yspec] * 4 + [wspec, row, vec, vec],
        out_specs=row,
        out_shape=jax.ShapeDtypeStruct((T, D), F32),
        scratch_shapes=[pltpu.VMEM((tm, len(ys) * G), BF16)],
        compiler_params=_cparams(("arbitrary",)),
        name="outproj_ln",
    )(*ys, w, x, g.reshape(1, D), b.reshape(1, D))


def _ffn_kernel(x_ref, wg_ref, wu_ref, wd_ref, g_ref, b_ref, o_ref, xb_ref, *, tn):
    f = pl.program_id(1)
    nf = pl.num_programs(1)

    tm = o_ref.shape[0]
    tr = _pick(tm, (256, 128))

    def rows(c):
        return pl.ds(pl.multiple_of(c * tr, tr), tr)

    @pl.when(f == 0)
    def _():
        def body(c, carry):
            xr = x_ref[rows(c), :]
            xb_ref[rows(c), :] = xr.astype(BF16)
            o_ref[rows(c), :] = DEEPNORM_ALPHA * xr
            return carry
        lax.fori_loop(0, tm // tr, body, 0)

    xb = xb_ref[...]
    gate = jnp.dot(xb, wg_ref[...], preferred_element_type=F32)
    upv = jnp.dot(xb, wu_ref[...], preferred_element_type=F32)
    hid = (gate * _sigmoid(gate) * upv).astype(BF16)
    for n in range(0, o_ref.shape[1], tn):
        o_ref[:, n:n + tn] += jnp.dot(hid, wd_ref[:, n:n + tn], preferred_element_type=F32)

    @pl.when(f == nf - 1)
    def _():
        def body(c, carry):
            o_ref[rows(c), :] = _layer_norm(o_ref[rows(c), :], g_ref[...], b_ref[...], LN_EPS)
            return carry
        lax.fori_loop(0, tm // tr, body, 0)


def _ffn_ln(x, wg, wu, wd, g, b):
    T, D = x.shape
    F = wg.shape[1]
    tm = _pick(T, (1024, 512, 256, 128))
    tf = _pick(F, (256, 128))
    row = pl.BlockSpec((tm, D), lambda i, f: (i, 0), pipeline_mode=pl.Buffered(1))
    row_once = row
    vec = pl.BlockSpec((1, D), lambda i, f: (0, 0))
    return pl.pallas_call(
        functools.partial(_ffn_kernel, tn=_pick(D, (512, 256, 128))),
        grid=(T // tm, F // tf),
        in_specs=[row_once, pl.BlockSpec((D, tf), lambda i, f: (0, f)), pl.BlockSpec((D, tf), lambda i, f: (0, f)),
                  pl.BlockSpec((tf, D), lambda i, f: (f, 0)), vec, vec],
        out_specs=row,
        out_shape=jax.ShapeDtypeStruct((T, D), F32),
        scratch_shapes=[pltpu.VMEM((tm, D), BF16)],
        compiler_params=_cparams(("arbitrary", "arbitrary")),
        name="ffn_ln",
    )(x, wg, wu, wd, g.reshape(1, D), b.reshape(1, D))


def _hybrid_layer(x, v_first, p):
    T, D = x.shape
    G = D // 4
    w_in = p['w_in']
    n_small = w_in.shape[1] - 9 * G
    small_w = _round_up(n_small, 2 * LANES)
    assert (9 * G) % small_w == 0
    w_head = w_in[:, :6 * G].astype(BF16)
    w_tail = jnp.concatenate(
        [w_in[:, 6 * G + n_small:], w_in[:, 6 * G:6 * G + n_small],
         jnp.zeros((D, small_w - n_small), w_in.dtype)], axis=1).astype(BF16)
    proj = _inproj(x, w_head, w_tail)
    y_conv, y_sgu, y_pool = _local_mixers(proj, G, p)
    y_rwkv, v_first = _rwkv_mixer(proj, G, small_w, 9 * G, p, v_first)
    x = _outproj_ln((y_conv, y_rwkv, y_sgu, y_pool), p['w_out'].astype(BF16), x, p['ln_mix_g'], p['ln_mix_b'])
    x = _ffn_ln(x, p['ffn_gate'].astype(BF16), p['ffn_up'].astype(BF16), p['ffn_down'].astype(BF16),
                p['ln_ffn_g'], p['ln_ffn_b'])
    return x, v_first


_NAMES_0 = ('w_in', 'conv_w', 'shift_mu', 'decay_w0', 'decay_up', 'iclr_a0', 'iclr_up', 'gate_up',
            'k_k', 'k_a', 'r_k', 'lnx_g', 'lnx_b', 'sgu_ln_g', 'sgu_ln_b', 'sgu_w', 'sgu_b',
            'pool_w', 'pool_scale', 'w_out', 'ln_mix_g', 'ln_mix_b', 'ffn_gate', 'ffn_up', 'ffn_down',
            'ln_ffn_g', 'ln_ffn_b')
_NAMES_1 = _NAMES_0[:7] + ('vres_v0', 'vres_up') + _NAMES_0[7:]


def kernel(x, w_in_0, conv_w_0, shift_mu_0, decay_w0_0, decay_up_0, iclr_a0_0, iclr_up_0, gate_up_0, k_k_0, k_a_0, r_k_0, lnx_g_0, lnx_b_0, sgu_ln_g_0, sgu_ln_b_0, sgu_w_0, sgu_b_0, pool_w_0, pool_scale_0, w_out_0, ln_mix_g_0, ln_mix_b_0, ffn_gate_0, ffn_up_0, ffn_down_0, ln_ffn_g_0, ln_ffn_b_0, w_in_1, conv_w_1, shift_mu_1, decay_w0_1, decay_up_1, iclr_a0_1, iclr_up_1, vres_v0_1, vres_up_1, gate_up_1, k_k_1, k_a_1, r_k_1, lnx_g_1, lnx_b_1, sgu_ln_g_1, sgu_ln_b_1, sgu_w_1, sgu_b_1, pool_w_1, pool_scale_1, w_out_1, ln_mix_g_1, ln_mix_b_1, ffn_gate_1, ffn_up_1, ffn_down_1, ln_ffn_g_1, ln_ffn_b_1):
    p0 = dict(zip(_NAMES_0, (w_in_0, conv_w_0, shift_mu_0, decay_w0_0, decay_up_0, iclr_a0_0, iclr_up_0, gate_up_0, k_k_0, k_a_0, r_k_0, lnx_g_0, lnx_b_0, sgu_ln_g_0, sgu_ln_b_0, sgu_w_0, sgu_b_0, pool_w_0, pool_scale_0, w_out_0, ln_mix_g_0, ln_mix_b_0, ffn_gate_0, ffn_up_0, ffn_down_0, ln_ffn_g_0, ln_ffn_b_0)))
    p1 = dict(zip(_NAMES_1, (w_in_1, conv_w_1, shift_mu_1, decay_w0_1, decay_up_1, iclr_a0_1, iclr_up_1, vres_v0_1, vres_up_1, gate_up_1, k_k_1, k_a_1, r_k_1, lnx_g_1, lnx_b_1, sgu_ln_g_1, sgu_ln_b_1, sgu_w_1, sgu_b_1, pool_w_1, pool_scale_1, w_out_1, ln_mix_g_1, ln_mix_b_1, ffn_gate_1, ffn_up_1, ffn_down_1, ln_ffn_g_1, ln_ffn_b_1)))
    B, T, D = x.shape
    assert B == 1
    h = x.reshape(T, D)
    h, v_first = _hybrid_layer(h, None, p0)
    h, _ = _hybrid_layer(h, v_first, p1)
    return h.reshape(B, T, D)
```

```python
import functools

import jax
import jax.numpy as jnp
from jax import lax
from jax.experimental import pallas as pl
from jax.experimental.pallas import tpu as pltpu

LANES = 128
BF16_SUBLANES = 16
RWKV_HEAD = 64
SGU_CHUNK = 128
SGU_HEAD = 128
CONV_W = 3
POOL_WINDOWS = (2, 4, 8, 16)
POOL_HALO = 16
CONV_HALO = 8
LN_EPS = 1e-5
RWKV_GN_EPS = 64e-5
WKV_CHUNK = 64
RWKV_BLOCK_LANES = 1024
DEPTH = 2
DEEPNORM_ALPHA = (2 * DEPTH) ** 0.25
VMEM_LIMIT_BYTES = 60 * 1024 * 1024

F32 = jnp.float32
BF16 = jnp.bfloat16


def _cparams(sem):
    return pltpu.CompilerParams(dimension_semantics=sem, vmem_limit_bytes=VMEM_LIMIT_BYTES)


def _pick(n, prefs):
    for p in prefs:
        if n % p == 0:
            return p
    return n


def _round_up(n, m):
    return (n + m - 1) // m * m


def _bdot(a, b):
    return jnp.dot(a.astype(BF16), b.astype(BF16), preferred_element_type=F32)


def _layer_norm(x, g, b, eps):
    mu = jnp.mean(x, axis=-1, keepdims=True)
    xc = x - mu
    var = jnp.mean(xc * xc, axis=-1, keepdims=True)
    return xc * lax.rsqrt(var + eps) * g + b


def _sigmoid(x):
    return 1.0 / (1.0 + jnp.exp(-x))


def _gelu_tanh(x):
    c = 0.7978845608028654
    return 0.5 * x * (1.0 + jnp.tanh(c * (x + 0.044715 * (x * x * x))))


def _softplus(x):
    return jnp.maximum(x, 0.0) + jnp.log(1.0 + jnp.exp(-jnp.abs(x)))


def _inproj_kernel(x_ref, w_ref, o_ref, xb_ref):
    @pl.when(pl.program_id(1) == 0)
    def _():
        xb_ref[...] = x_ref[...].astype(BF16)

    o_ref[...] = jnp.dot(xb_ref[...], w_ref[...], preferred_element_type=F32)


def _inproj(x, w):
    T, D = x.shape
    N = w.shape[1]
    tm = _pick(T, (1024, 512, 256, 128))
    tn = _pick(N, (512, 256, 128))
    return pl.pallas_call(
        _inproj_kernel,
        grid=(T // tm, N // tn),
        in_specs=[pl.BlockSpec((tm, D), lambda i, j: (i, 0)),
                  pl.BlockSpec((D, tn), lambda i, j: (0, j))],
        out_specs=pl.BlockSpec((tm, tn), lambda i, j: (i, j)),
        out_shape=jax.ShapeDtypeStruct((T, N), F32),
        scratch_shapes=[pltpu.VMEM((tm, D), BF16)],
        compiler_params=_cparams(("arbitrary", "arbitrary")),
        name="inproj",
    )(x, w)


def _local_mixers_kernel(h_ref, bg_ref, cg_ref, su_ref, sv_ref, pz_ref,
                         convw_ref, lng_ref, lnb_ref, sguw_ref, sgubt_ref, poolw_ref, pools_ref,
                         yconv_ref, ysgu_ref, ypool_ref,
                         zbuf_ref, pbuf_ref, *, tb):
    i = pl.program_id(0)

    @pl.when(i == 0)
    def _():
        zbuf_ref[0:CONV_HALO, :] = jnp.zeros((CONV_HALO, zbuf_ref.shape[1]), F32)
        pbuf_ref[0:POOL_HALO, :] = jnp.zeros((POOL_HALO, pbuf_ref.shape[1]), F32)

    zbuf_ref[CONV_HALO:CONV_HALO + tb, :] = cg_ref[...] * h_ref[...]
    conv = zbuf_ref[CONV_HALO - (CONV_W - 1):CONV_HALO - (CONV_W - 1) + tb, :] * convw_ref[0:1, :]
    for j in range(1, CONV_W):
        off = CONV_HALO - (CONV_W - 1 - j)
        conv = conv + zbuf_ref[off:off + tb, :] * convw_ref[j:j + 1, :]
    yconv_ref[...] = (bg_ref[...] * conv).astype(yconv_ref.dtype)
    zbuf_ref[0:CONV_HALO, :] = zbuf_ref[tb:tb + CONV_HALO, :]

    u = _gelu_tanh(su_ref[...])
    v = _layer_norm(_gelu_tanh(sv_ref[...]), lng_ref[...], lnb_ref[...], LN_EPS).astype(BF16)
    n_heads = sguw_ref.shape[0]
    row = lax.broadcasted_iota(jnp.int32, (SGU_CHUNK, SGU_CHUNK), 0)
    col = lax.broadcasted_iota(jnp.int32, (SGU_CHUNK, SGU_CHUNK), 1)
    causal = col <= row
    for hd in range(n_heads):
        w_h = jnp.where(causal, sguw_ref[hd], 0.0).astype(BF16)
        bias = sgubt_ref[:, hd:hd + 1]
        cs = slice(hd * SGU_HEAD, (hd + 1) * SGU_HEAD)
        for c in range(tb // SGU_CHUNK):
            rs = slice(c * SGU_CHUNK, (c + 1) * SGU_CHUNK)
            s = jnp.dot(w_h, v[rs, cs], preferred_element_type=F32) + bias
            ysgu_ref[rs, cs] = (u[rs, cs] * s).astype(ysgu_ref.dtype)

    pbuf_ref[POOL_HALO:POOL_HALO + tb, :] = pz_ref[...]
    pg = poolw_ref.shape[1]
    t_glob = i * tb + lax.broadcasted_iota(jnp.int32, (tb, 1), 0)
    for gi, win in enumerate(POOL_WINDOWS):
        cs = slice(gi * pg, (gi + 1) * pg)
        z = pbuf_ref[POOL_HALO:POOL_HALO + tb, cs]
        acc = z
        for j in range(1, win):
            acc = acc + pbuf_ref[POOL_HALO - j:POOL_HALO - j + tb, cs]
        cnt = jnp.minimum(t_glob + 1, win).astype(F32)
        d = acc / cnt - z
        y = _bdot(d, poolw_ref[gi])
        ypool_ref[:, cs] = (y * pools_ref[:, cs]).astype(ypool_ref.dtype)
    pbuf_ref[0:POOL_HALO, :] = pbuf_ref[tb:tb + POOL_HALO, :]


def _local_mixers(proj, G, p):
    T = proj.shape[0]
    tb = _pick(T, (256, 128))
    n_sgu = G // SGU_HEAD
    col = lambda c: pl.BlockSpec((tb, G), lambda i, c=c: (i, c))
    full = lambda a: pl.BlockSpec(a.shape, lambda i, n=a.ndim: (0,) * n)
    convw = p['conv_w']
    lng = p['sgu_ln_g'].reshape(1, G)
    lnb = p['sgu_ln_b'].reshape(1, G)
    sguw = p['sgu_w']
    sgubt = p['sgu_b'].T
    poolw = p['pool_w'].astype(BF16)
    pools = p['pool_scale'].reshape(1, G)
    outs = pl.pallas_call(
        functools.partial(_local_mixers_kernel, tb=tb),
        grid=(T // tb,),
        in_specs=[col(0), col(1), col(2), col(6), col(7), col(8),
                  full(convw), full(lng), full(lnb), full(sguw), full(sgubt), full(poolw), full(pools)],
        out_specs=[pl.BlockSpec((tb, G), lambda i: (i, 0))] * 3,
        out_shape=[jax.ShapeDtypeStruct((T, G), BF16)] * 3,
        scratch_shapes=[pltpu.VMEM((tb + CONV_HALO, G), F32), pltpu.VMEM((tb + POOL_HALO, G), F32)],
        compiler_params=_cparams(("arbitrary",)),
        name="local_mixers",
    )(proj, proj, proj, proj, proj, proj, convw, lng, lnb, sguw, sgubt, poolw, pools)
    assert n_sgu == sguw.shape[0]
    return outs


def _each(fn, *lists):
    return [fn(*args) for args in zip(*lists)]


def _unit_lower_inverse(Ls, eye, row, col):
    blk16 = (row >> 4) == (col >> 4)
    blk32 = (row >> 5) == (col >> 5)
    P = [jnp.where(blk16, L, 0.0) for L in Ls]
    T = [eye + p for p in P]
    for _ in range(3):
        P = _each(_pdot, P, P)
        T = _each(lambda t, p: t + _pdot(t, p), T, P)
    for off_diag in (blk32 & jnp.logical_not(blk16), jnp.logical_not(blk32)):
        X = _each(lambda L, t: _pdot(jnp.where(off_diag, L, 0.0), t), Ls, T)
        T = _each(lambda t, x: t + _pdot(t, x), T, X)
    return T


def _block_diag(x):
    x = x.astype(BF16)
    first = lax.broadcasted_iota(jnp.int32, x.shape, 1) < RWKV_HEAD
    zero = jnp.zeros_like(x)
    return jnp.concatenate([jnp.where(first, x, zero), jnp.where(first, zero, x)], axis=0)


def _fold_diag(x):
    first = lax.broadcasted_iota(jnp.int32, (RWKV_HEAD, LANES), 1) < RWKV_HEAD
    return jnp.where(first, x[:RWKV_HEAD], x[RWKV_HEAD:])


def _pdot(a, b):
    return jnp.dot(a.astype(BF16), _block_diag(b), preferred_element_type=F32)


def _pdot_nt(a, b):
    return lax.dot_general(a.astype(BF16), _block_diag(b), (((1,), (1,)), ((), ())), preferred_element_type=F32)


def _pdot_tn(a, b):
    return _fold_diag(_bdot_tn(a, b))


def _bdot_tn(a, b):
    return lax.dot_general(a.astype(BF16), b.astype(BF16), (((0,), (0,)), ((), ())), preferred_element_type=F32)


def _wkv_chunk_operators(at, rt, bt, kt, bh, kh, v):
    C = at[0].shape[0]
    assert C == RWKV_HEAD
    row = lax.broadcasted_iota(jnp.int32, (C, LANES), 0)
    col = lax.broadcasted_iota(jnp.int32, (C, LANES), 1) & (RWKV_HEAD - 1)
    incl = col <= row
    strict = col < row
    eye = jnp.where(row == col, 1.0, 0.0).astype(F32)
    ar = _each(lambda x, y: jnp.concatenate([x, y.astype(BF16)], axis=0), at, rt)
    p_b = _each(_pdot_nt, ar, bt)
    p_k = _each(_pdot_nt, ar, kt)
    Lab = [jnp.where(strict, p[:C], 0.0) for p in p_b]
    Mak = [jnp.where(strict, p[:C], 0.0) for p in p_k]
    Mrb = [jnp.where(incl, p[C:], 0.0).astype(BF16) for p in p_b]
    Mrk = [jnp.where(incl, p[C:], 0.0) for p in p_k]
    MakV = _each(_pdot, Mak, v)
    MrkV = _each(_pdot, Mrk, v)
    KV = _each(_pdot_tn, v, kh)
    Tinv = [t.astype(BF16) for t in _unit_lower_inverse(Lab, eye, row, col)]
    W = [w.astype(BF16) for w in _each(_pdot, Tinv, at)]
    U = [u.astype(BF16) for u in _each(_pdot, Tinv, MakV)]
    Q = _each(lambda x, m, w: x + _pdot(m, w), rt, Mrb, W)
    y_add = _each(lambda m, u, mv: _pdot(m, u) + mv, Mrb, U, MrkV)
    m_state = _each(_pdot_tn, W, bh)
    s_add = _each(lambda u, b_, kv: _pdot_tn(u, b_) + kv, U, bh, KV)
    return Q, y_add, m_state, s_add


def _rwkv_kernel(*refs, tb, has_vres, cast_moves):
    n_cast = len(cast_moves)
    n_in = 21 if has_vres else 18
    n_out = 1 if has_vres else 2
    ins, refs = refs[:n_in], refs[n_in:]
    cast_in, refs = refs[:n_cast], refs[n_cast:]
    outs, refs = refs[:n_out], refs[n_out:]
    cast_out, refs = refs[:n_cast], refs[n_cast:]
    prev_ref, prevs_ref, S_ref, ybuf_ref = refs
    if has_vres:
        (r_ref, k_ref, v_ref, sm_ref, vfirst_ref,
         mur_ref, muk_ref, muv_ref, mus_ref,
         w0_ref, wup_ref, a0_ref, aup_ref, gup_ref, v0_ref, vup_ref,
         kk_ref, ka_ref, rk_ref, lng_ref, lnb_ref) = ins
        (y_ref,) = outs
    else:
        (r_ref, k_ref, v_ref, sm_ref,
         mur_ref, muk_ref, muv_ref, mus_ref,
         w0_ref, wup_ref, a0_ref, aup_ref, gup_ref,
         kk_ref, ka_ref, rk_ref, lng_ref, lnb_ref) = ins
        y_ref, vfirst_out_ref = outs
    t = pl.program_id(1)

    for src_ref, dst_ref, move in zip(cast_in, cast_out, cast_moves):
        if move is None:
            dst_ref[...] = src_ref[...].astype(BF16)
        else:
            head, n = move
            src = src_ref[...].astype(BF16)
            rest = src.shape[1] - head - n
            dst_ref[:, 0:head] = src[:, 0:head]
            dst_ref[:, head:head + rest] = src[:, head + n:]
            dst_ref[:, head + rest:] = jnp.zeros((src.shape[0], dst_ref.shape[1] - head - rest), BF16)
            dst_ref[:, head + rest:head + rest + n] = src[:, head:head + n]

    @pl.when(t == 0)
    def _():
        prev_ref[...] = jnp.zeros(prev_ref.shape, F32)
        prevs_ref[...] = jnp.zeros(prevs_ref.shape, F32)
        S_ref[...] = jnp.zeros(S_ref.shape, F32)

    def shift_mix(raw, prev_row, mu):
        first = lax.broadcasted_iota(jnp.int32, raw.shape, 0) == 0
        sh = jnp.where(first, prev_row, pltpu.roll(raw, 1, 0))
        return raw + (sh - raw) * mu

    r_raw, k_raw, v_raw, sm_raw = r_ref[...], k_ref[...], v_ref[...], sm_ref[...]
    r = shift_mix(r_raw, prev_ref[0, 7:8, :], mur_ref[...])
    k = shift_mix(k_raw, prev_ref[1, 7:8, :], muk_ref[...])
    v = shift_mix(v_raw, prev_ref[2, 7:8, :], muv_ref[...])
    sm = shift_mix(sm_raw, prevs_ref[7:8, :], mus_ref[...])
    prev_ref[0] = r_raw[tb - 8:tb, :]
    prev_ref[1] = k_raw[tb - 8:tb, :]
    prev_ref[2] = v_raw[tb - 8:tb, :]
    prevs_ref[...] = sm_raw[tb - 8:tb, :]

    dl, al, gl = wup_ref.shape[0], aup_ref.shape[0], gup_ref.shape[0]
    wd = sm[:, 0:dl]
    ad = sm[:, dl:dl + al]
    gd = sm[:, dl + al:dl + al + gl]
    w = -_softplus(-(w0_ref[...] + _bdot(jnp.tanh(wd), wup_ref[...]))) - 0.5
    lw = -jnp.exp(w)
    a = _sigmoid(a0_ref[...] + _bdot(ad, aup_ref[...]))
    g = _bdot(_sigmoid(gd), gup_ref[...])
    if has_vres:
        ml = vup_ref.shape[0]
        vd = sm[:, dl + al + gl:dl + al + gl + ml]
        v = v + (vfirst_ref[...] - v) * _sigmoid(v0_ref[...] + _bdot(vd, vup_ref[...]))
    else:
        vfirst_out_ref[...] = v

    li = lax.broadcasted_iota(jnp.int32, (LANES, LANES), 0) // RWKV_HEAD
    lj = lax.broadcasted_iota(jnp.int32, (LANES, LANES), 1) // RWKV_HEAD
    head_ones = jnp.where(li == lj, 1.0, 0.0).astype(BF16)
    width = y_ref.shape[1]
    n_pairs = width // LANES

    def split3(z):
        hi = z.astype(BF16)
        r1 = z - hi.astype(F32)
        mid = r1.astype(BF16)
        return hi, mid, (r1 - mid.astype(F32)).astype(BF16)

    def head_sum(z):
        parts = split3(z)
        return jnp.concatenate(
            [sum(jnp.dot(q[:, j * LANES:(j + 1) * LANES], head_ones, preferred_element_type=F32) for q in parts)
             for j in range(n_pairs)], axis=1)

    kk = k * kk_ref[...]
    kk = kk / jnp.maximum(jnp.sqrt(head_sum(kk * kk)), 1e-12)
    k = k * (1.0 + (a - 1.0) * ka_ref[...])

    C = WKV_CHUNK
    n_chunks = tb // C
    ti = lax.broadcasted_iota(jnp.int32, (tb, tb), 0)
    tj = lax.broadcasted_iota(jnp.int32, (tb, tb), 1)
    tri = jnp.where((tj <= ti) & ((ti // C) == (tj // C)), 1.0, 0.0).astype(BF16)
    lc = sum(jnp.dot(tri, q, preferred_element_type=F32) for q in split3(lw))
    lc_end = jnp.concatenate(
        [jnp.broadcast_to(lc[(c + 1) * C - 1:(c + 1) * C, :], (C, width)) for c in range(n_chunks)], axis=0)
    e_neg = jnp.exp(-lc)
    e_end = jnp.exp(lc_end - lc)
    at = (-kk * jnp.exp(lc - lw)).astype(BF16)
    rt = r * jnp.exp(lc)
    bt = (kk * a * e_neg).astype(BF16)
    kt = (k * e_neg).astype(BF16)
    bh = (kk * a * e_end).astype(BF16)
    kh = (k * e_end).astype(BF16)
    vb = v.astype(BF16)
    decay_end = jnp.exp(lc_end)

    tiles = [(c, j) for c in range(n_chunks) for j in range(n_pairs)]
    cut = lambda z: [z[c * C:(c + 1) * C, j * LANES:(j + 1) * LANES] for c, j in tiles]
    Q, y_add, m_state, s_add = _wkv_chunk_operators(*[cut(z) for z in (at, rt, bt, kt, bh, kh, vb)])
    S = [S_ref[j] for j in range(n_pairs)]
    for c in range(n_chunks):
        for j in range(n_pairs):
            i = c * n_pairs + j
            ls = slice(j * LANES, (j + 1) * LANES)
            ybuf_ref[c * C:(c + 1) * C, ls] = _pdot_nt(Q[i], S[j]) + y_add[i]
            S[j] = S[j] * decay_end[c * C:c * C + 1, ls] + _pdot(S[j], m_state[i]) + s_add[i]
    for j in range(n_pairs):
        S_ref[j] = S[j]

    y = ybuf_ref[...]
    inv_n = 1.0 / RWKV_HEAD
    mu = head_sum(y) * inv_n
    yc = y - mu
    var = head_sum(yc * yc) * inv_n
    y = yc * lax.rsqrt(var + RWKV_GN_EPS) * lng_ref[...] + lnb_ref[...]
    y = y + head_sum(r * k * rk_ref[...]) * v
    y_ref[...] = (y * g).astype(y_ref.dtype)


def _cast_plan(shape, n_steps):
    rows, cols = shape
    for ncb in (1, 2, 4, 8, 16):
        rb = n_steps // ncb
        if (n_steps % ncb == 0 and rows % rb == 0 and cols % ncb == 0
                and (rows // rb) % BF16_SUBLANES == 0 and (cols // ncb) % LANES == 0):
            return rb, ncb
    return None


def _permuted_w_in(w_in, head, n, width):
    return jnp.concatenate([w_in[:, :head], w_in[:, head + n:], w_in[:, head:head + n],
                            jnp.zeros((w_in.shape[0], width - w_in.shape[1]), w_in.dtype)], axis=1).astype(BF16)


def _rwkv_mixer(proj, G, small_w, small_off, p, v_first, to_cast, next_w_in):
    T = proj.shape[0]
    has_vres = v_first is not None
    tb = _pick(T, (256, 128, 64))
    bw = _pick(G, (RWKV_BLOCK_LANES, 2 * LANES, LANES))
    cb = G // bw
    dl, al, gl = p['decay_up'].shape[0], p['iclr_up'].shape[0], p['gate_up'].shape[0]
    ml = p['vres_up'].shape[0] if has_vres else 0
    mu = p['shift_mu']
    mu_r, mu_k, mu_v = (mu[q * G:(q + 1) * G].reshape(1, G) for q in range(3))
    mu_s = jnp.pad(mu[3 * G:], (0, small_w - (dl + al + gl + ml))).reshape(1, small_w)

    def colblk(first):
        return pl.BlockSpec((tb, bw), lambda q, t, f=first: (t, f + q))

    def vec(arr):
        return arr.reshape(1, G), pl.BlockSpec((1, bw), lambda q, t: (0, q))

    def up(arr):
        return arr.astype(BF16), pl.BlockSpec((arr.shape[0], bw), lambda q, t: (0, q))

    small_spec = pl.BlockSpec((tb, small_w), lambda q, t: (t, small_off // small_w))
    pair_spec = pl.BlockSpec((tb, bw), lambda q, t: (t, q))
    mu_spec = pl.BlockSpec((1, bw), lambda q, t: (0, q))
    args = [proj, proj, proj, proj]
    specs = [colblk(3 * cb), colblk(4 * cb), colblk(5 * cb), small_spec]
    if has_vres:
        args.append(v_first)
        specs.append(pair_spec)
    args += [mu_r, mu_k, mu_v, mu_s]
    specs += [mu_spec, mu_spec, mu_spec, pl.BlockSpec((1, small_w), lambda q, t: (0, 0))]
    names = ['decay_w0', 'decay_up', 'iclr_a0', 'iclr_up', 'gate_up']
    if has_vres:
        names += ['vres_v0', 'vres_up']
    names += ['k_k', 'k_a', 'r_k', 'lnx_g', 'lnx_b']
    for nm in names:
        arr, spec = up(p[nm]) if nm.endswith('_up') else vec(p[nm])
        args.append(arr)
        specs.append(spec)
    out_shape = [jax.ShapeDtypeStruct((T, G), BF16)]
    out_specs = [pair_spec]
    if not has_vres:
        out_shape.append(jax.ShapeDtypeStruct((T, G), F32))
        out_specs.append(pair_spec)
    n_t = T // tb
    n_steps = (G // bw) * n_t
    cast_names, cast_moves, cast_out = [], [], {}
    for nm, arr in to_cast.items():
        plan = _cast_plan(arr.shape, n_steps)
        if plan is None:
            cast_out[nm] = arr.astype(BF16)
            continue
        rb, ncb = plan
        spec = pl.BlockSpec((arr.shape[0] // rb, arr.shape[1] // ncb),
                            lambda q, t, ncb=ncb: ((q * n_t + t) // ncb, (q * n_t + t) % ncb))
        cast_names.append(nm)
        cast_moves.append(None)
        args.append(arr)
        specs.append(spec)
        out_shape.append(jax.ShapeDtypeStruct(arr.shape, BF16))
        out_specs.append(spec)
    if next_w_in is not None:
        w_next, head, n, width = next_w_in
        rows = w_next.shape[0]
        if rows % n_steps == 0 and (rows // n_steps) % BF16_SUBLANES == 0:
            row_blk = lambda q, t: (q * n_t + t, 0)
            cast_names.append('w_in_next')
            cast_moves.append((head, n))
            args.append(w_next)
            specs.append(pl.BlockSpec((rows // n_steps, w_next.shape[1]), row_blk))
            out_shape.append(jax.ShapeDtypeStruct((rows, width), BF16))
            out_specs.append(pl.BlockSpec((rows // n_steps, width), row_blk))
        else:
            cast_out['w_in_next'] = _permuted_w_in(w_next, head, n, width)
    n_main_out = 1 if has_vres else 2
    outs = pl.pallas_call(
        functools.partial(_rwkv_kernel, tb=tb, has_vres=has_vres, cast_moves=tuple(cast_moves)),
        grid=(G // bw, T // tb),
        in_specs=specs,
        out_specs=out_specs,
        out_shape=out_shape,
        scratch_shapes=[pltpu.VMEM((3, 8, bw), F32), pltpu.VMEM((8, small_w), F32),
                        pltpu.VMEM((bw // LANES, RWKV_HEAD, LANES), F32),
                        pltpu.VMEM((tb, bw), F32)],
        compiler_params=_cparams(("arbitrary", "arbitrary")),
        name="rwkv7_mixer",
    )(*args)
    cast_out.update(zip(cast_names, outs[n_main_out:]))
    return outs[0], (v_first if has_vres else outs[1]), cast_out


def _outproj_kernel(y0_ref, y1_ref, y2_ref, y3_ref, w_ref, x_ref, g_ref, b_ref, o_ref, ycat_ref, *, tn):
    G = y0_ref.shape[1]
    for j, y_ref in enumerate((y0_ref, y1_ref, y2_ref, y3_ref)):
        ycat_ref[:, j * G:(j + 1) * G] = y_ref[...]
    ycat = ycat_ref[...]
    for n in range(0, o_ref.shape[1], tn):
        o_ref[:, n:n + tn] = (DEEPNORM_ALPHA * x_ref[:, n:n + tn]
                              + jnp.dot(ycat, w_ref[:, n:n + tn], preferred_element_type=F32))
    o_ref[...] = _layer_norm(o_ref[...], g_ref[...], b_ref[...], LN_EPS)


def _outproj_ln(ys, w, x, g, b):
    T, D = x.shape
    G = ys[0].shape[1]
    tm = _pick(T, (256, 128))
    yspec = pl.BlockSpec((tm, G), lambda i: (i, 0))
    row = pl.BlockSpec((tm, D), lambda i: (i, 0))
    vec = pl.BlockSpec((1, D), lambda i: (0, 0))
    wspec = pl.BlockSpec(w.shape, lambda i: (0, 0), pipeline_mode=pl.Buffered(1))
    return pl.pallas_call(
        functools.partial(_outproj_kernel, tn=_pick(D, (512, 256, 128))),
        grid=(T // tm,),
        in_specs=[yspec] * 4 + [wspec, row, vec, vec],
        out_specs=row,
        out_shape=jax.ShapeDtypeStruct((T, D), F32),
        scratch_shapes=[pltpu.VMEM((tm, len(ys) * G), BF16)],
        compiler_params=_cparams(("arbitrary",)),
        name="outproj_ln",
    )(*ys, w, x, g.reshape(1, D), b.reshape(1, D))


def _ffn_kernel(x_ref, wg_ref, wu_ref, wd_ref, g_ref, b_ref, o_ref, xb_ref, *, tn):
    f = pl.program_id(1)
    nf = pl.num_programs(1)

    tm = o_ref.shape[0]
    tr = _pick(tm, (256, 128))

    def rows(c):
        return pl.ds(pl.multiple_of(c * tr, tr), tr)

    @pl.when(f == 0)
    def _():
        def body(c, carry):
            xr = x_ref[rows(c), :]
            xb_ref[rows(c), :] = xr.astype(BF16)
            o_ref[rows(c), :] = DEEPNORM_ALPHA * xr
            return carry
        lax.fori_loop(0, tm // tr, body, 0)

    xb = xb_ref[...]
    gate = jnp.dot(xb, wg_ref[...], preferred_element_type=F32)
    upv = jnp.dot(xb, wu_ref[...], preferred_element_type=F32)
    hid = (gate * _sigmoid(gate) * upv).astype(BF16)
    for n in range(0, o_ref.shape[1], tn):
        o_ref[:, n:n + tn] += jnp.dot(hid, wd_ref[:, n:n + tn], preferred_element_type=F32)

    @pl.when(f == nf - 1)
    def _():
        def body(c, carry):
            o_ref[rows(c), :] = _layer_norm(o_ref[rows(c), :], g_ref[...], b_ref[...], LN_EPS)
            return carry
        lax.fori_loop(0, tm // tr, body, 0)


def _ffn_ln(x, wg, wu, wd, g, b):
    T, D = x.shape
    F = wg.shape[1]
    tm = _pick(T, (1024, 512, 256, 128))
    tf = _pick(F, (256, 128))
    row = pl.BlockSpec((tm, D), lambda i, f: (i, 0), pipeline_mode=pl.Buffered(1))
    row_once = row
    vec = pl.BlockSpec((1, D), lambda i, f: (0, 0))
    return pl.pallas_call(
        functools.partial(_ffn_kernel, tn=_pick(D, (512, 256, 128))),
        grid=(T // tm, F // tf),
        in_specs=[row_once, pl.BlockSpec((D, tf), lambda i, f: (0, f)), pl.BlockSpec((D, tf), lambda i, f: (0, f)),
                  pl.BlockSpec((tf, D), lambda i, f: (f, 0)), vec, vec],
        out_specs=row,
        out_shape=jax.ShapeDtypeStruct((T, D), F32),
        scratch_shapes=[pltpu.VMEM((tm, D), BF16)],
        compiler_params=_cparams(("arbitrary", "arbitrary")),
        name="ffn_ln",
    )(x, wg, wu, wd, g.reshape(1, D), b.reshape(1, D))


def _w_in_layout(w_in, G):
    n_small = w_in.shape[1] - 9 * G
    small_w = _round_up(n_small, 2 * LANES)
    assert (9 * G) % small_w == 0
    return 6 * G, n_small, 9 * G + small_w


def _hybrid_layer(x, v_first, p, w_in_b, next_w_in):
    T, D = x.shape
    G = D // 4
    small_w = w_in_b.shape[1] - 9 * G
    proj = _inproj(x, w_in_b)
    y_conv, y_sgu, y_pool = _local_mixers(proj, G, p)
    later = ('w_out', 'ffn_gate', 'ffn_up', 'ffn_down')
    nxt = None if next_w_in is None else (next_w_in,) + _w_in_layout(next_w_in, G)
    y_rwkv, v_first, wb = _rwkv_mixer(proj, G, small_w, 9 * G, p, v_first, {nm: p[nm] for nm in later}, nxt)
    x = _outproj_ln((y_conv, y_rwkv, y_sgu, y_pool), wb['w_out'], x, p['ln_mix_g'], p['ln_mix_b'])
    x = _ffn_ln(x, wb['ffn_gate'], wb['ffn_up'], wb['ffn_down'], p['ln_ffn_g'], p['ln_ffn_b'])
    return x, v_first, wb.get('w_in_next')


_NAMES_0 = ('w_in', 'conv_w', 'shift_mu', 'decay_w0', 'decay_up', 'iclr_a0', 'iclr_up', 'gate_up',
            'k_k', 'k_a', 'r_k', 'lnx_g', 'lnx_b', 'sgu_ln_g', 'sgu_ln_b', 'sgu_w', 'sgu_b',
            'pool_w', 'pool_scale', 'w_out', 'ln_mix_g', 'ln_mix_b', 'ffn_gate', 'ffn_up', 'ffn_down',
            'ln_ffn_g', 'ln_ffn_b')
_NAMES_1 = _NAMES_0[:7] + ('vres_v0', 'vres_up') + _NAMES_0[7:]


def kernel(x, w_in_0, conv_w_0, shift_mu_0, decay_w0_0, decay_up_0, iclr_a0_0, iclr_up_0, gate_up_0, k_k_0, k_a_0, r_k_0, lnx_g_0, lnx_b_0, sgu_ln_g_0, sgu_ln_b_0, sgu_w_0, sgu_b_0, pool_w_0, pool_scale_0, w_out_0, ln_mix_g_0, ln_mix_b_0, ffn_gate_0, ffn_up_0, ffn_down_0, ln_ffn_g_0, ln_ffn_b_0, w_in_1, conv_w_1, shift_mu_1, decay_w0_1, decay_up_1, iclr_a0_1, iclr_up_1, vres_v0_1, vres_up_1, gate_up_1, k_k_1, k_a_1, r_k_1, lnx_g_1, lnx_b_1, sgu_ln_g_1, sgu_ln_b_1, sgu_w_1, sgu_b_1, pool_w_1, pool_scale_1, w_out_1, ln_mix_g_1, ln_mix_b_1, ffn_gate_1, ffn_up_1, ffn_down_1, ln_ffn_g_1, ln_ffn_b_1):
    p0 = dict(zip(_NAMES_0, (w_in_0, conv_w_0, shift_mu_0, decay_w0_0, decay_up_0, iclr_a0_0, iclr_up_0, gate_up_0, k_k_0, k_a_0, r_k_0, lnx_g_0, lnx_b_0, sgu_ln_g_0, sgu_ln_b_0, sgu_w_0, sgu_b_0, pool_w_0, pool_scale_0, w_out_0, ln_mix_g_0, ln_mix_b_0, ffn_gate_0, ffn_up_0, ffn_down_0, ln_ffn_g_0, ln_ffn_b_0)))
    p1 = dict(zip(_NAMES_1, (w_in_1, conv_w_1, shift_mu_1, decay_w0_1, decay_up_1, iclr_a0_1, iclr_up_1, vres_v0_1, vres_up_1, gate_up_1, k_k_1, k_a_1, r_k_1, lnx_g_1, lnx_b_1, sgu_ln_g_1, sgu_ln_b_1, sgu_w_1, sgu_b_1, pool_w_1, pool_scale_1, w_out_1, ln_mix_g_1, ln_mix_b_1, ffn_gate_1, ffn_up_1, ffn_down_1, ln_ffn_g_1, ln_ffn_b_1)))
    B, T, D = x.shape
    assert B == 1
    h = x.reshape(T, D)
    w_in_b = _permuted_w_in(w_in_0, *_w_in_layout(w_in_0, D // 4))
    h, v_first, w_in_b = _hybrid_layer(h, None, p0, w_in_b, w_in_1)
    h, _, _ = _hybrid_layer(h, v_first, p1, w_in_b, None)
    return h.reshape(B, T, D)
```

```python
import functools

import jax
import jax.numpy as jnp
from jax import lax
from jax.experimental import pallas as pl
from jax.experimental.pallas import tpu as pltpu

LANES = 128
BF16_SUBLANES = 16
RWKV_HEAD = 64
SGU_CHUNK = 128
SGU_HEAD = 128
CONV_W = 3
POOL_WINDOWS = (2, 4, 8, 16)
POOL_HALO = 16
CONV_HALO = 8
LN_EPS = 1e-5
RWKV_GN_EPS = 64e-5
WKV_CHUNK = 64
RWKV_BLOCK_LANES = 1024
DEPTH = 2
DEEPNORM_ALPHA = (2 * DEPTH) ** 0.25
VMEM_LIMIT_BYTES = 60 * 1024 * 1024

F32 = jnp.float32
BF16 = jnp.bfloat16


def _cparams(sem):
    return pltpu.CompilerParams(dimension_semantics=sem, vmem_limit_bytes=VMEM_LIMIT_BYTES)


def _pick(n, prefs):
    for p in prefs:
        if n % p == 0:
            return p
    return n


def _round_up(n, m):
    return (n + m - 1) // m * m


def _bdot(a, b):
    return jnp.dot(a.astype(BF16), b.astype(BF16), preferred_element_type=F32)


def _layer_norm(x, g, b, eps):
    mu = jnp.mean(x, axis=-1, keepdims=True)
    xc = x - mu
    var = jnp.mean(xc * xc, axis=-1, keepdims=True)
    return xc * lax.rsqrt(var + eps) * g + b


def _sigmoid(x):
    return 1.0 / (1.0 + jnp.exp(-x))


def _gelu_tanh(x):
    c = 0.7978845608028654
    return 0.5 * x * (1.0 + jnp.tanh(c * (x + 0.044715 * (x * x * x))))


def _softplus(x):
    return jnp.maximum(x, 0.0) + jnp.log(1.0 + jnp.exp(-jnp.abs(x)))


def _inproj_kernel(x_ref, wt_ref, o_ref, xb_ref):
    @pl.when(pl.program_id(1) == 0)
    def _():
        xb_ref[...] = x_ref[...].astype(BF16)

    o_ref[...] = lax.dot_general(xb_ref[...], wt_ref[...], (((1,), (1,)), ((), ())), preferred_element_type=F32)


def _inproj(x, wt, head, n_small, width):
    T, D = x.shape
    tm = _pick(T, (1024, 512, 256, 128))
    tail = width - (wt.shape[0] - n_small)
    tn = next(c for c in (512, 256, 128) if head % c == 0 and width % c == 0 and tail == c)
    n_head = head // tn
    n_blocks = width // tn

    assert n_small % BF16_SUBLANES == 0

    def row_start(i, j):
        moved = head + n_small + (j - n_head) * tn
        start = jnp.where(j < n_head, j * tn, jnp.where(j < n_blocks - 1, moved, head))
        return pl.multiple_of(start, BF16_SUBLANES), 0

    return pl.pallas_call(
        _inproj_kernel,
        grid=(T // tm, n_blocks),
        in_specs=[pl.BlockSpec((tm, D), lambda i, j: (i, 0)),
                  pl.BlockSpec((pl.Element(tn), pl.Element(D)), row_start)],
        out_specs=pl.BlockSpec((tm, tn), lambda i, j: (i, j)),
        out_shape=jax.ShapeDtypeStruct((T, width), F32),
        scratch_shapes=[pltpu.VMEM((tm, D), BF16)],
        compiler_params=_cparams(("arbitrary", "arbitrary")),
        name="inproj",
    )(x, wt)


def _local_mixers_kernel(h_ref, bg_ref, cg_ref, su_ref, sv_ref, pz_ref,
                         convw_ref, lng_ref, lnb_ref, sguw_ref, sgubt_ref, poolw_ref, pools_ref,
                         yconv_ref, ysgu_ref, ypool_ref,
                         zbuf_ref, pbuf_ref, *, tb):
    i = pl.program_id(0)

    @pl.when(i == 0)
    def _():
        zbuf_ref[0:CONV_HALO, :] = jnp.zeros((CONV_HALO, zbuf_ref.shape[1]), F32)
        pbuf_ref[0:POOL_HALO, :] = jnp.zeros((POOL_HALO, pbuf_ref.shape[1]), F32)

    zbuf_ref[CONV_HALO:CONV_HALO + tb, :] = cg_ref[...] * h_ref[...]
    conv = zbuf_ref[CONV_HALO - (CONV_W - 1):CONV_HALO - (CONV_W - 1) + tb, :] * convw_ref[0:1, :]
    for j in range(1, CONV_W):
        off = CONV_HALO - (CONV_W - 1 - j)
        conv = conv + zbuf_ref[off:off + tb, :] * convw_ref[j:j + 1, :]
    yconv_ref[...] = (bg_ref[...] * conv).astype(yconv_ref.dtype)
    zbuf_ref[0:CONV_HALO, :] = zbuf_ref[tb:tb + CONV_HALO, :]

    u = _gelu_tanh(su_ref[...])
    v = _layer_norm(_gelu_tanh(sv_ref[...]), lng_ref[...], lnb_ref[...], LN_EPS).astype(BF16)
    n_heads = sguw_ref.shape[0]
    row = lax.broadcasted_iota(jnp.int32, (SGU_CHUNK, SGU_CHUNK), 0)
    col = lax.broadcasted_iota(jnp.int32, (SGU_CHUNK, SGU_CHUNK), 1)
    causal = col <= row
    for hd in range(n_heads):
        w_h = jnp.where(causal, sguw_ref[hd], 0.0).astype(BF16)
        bias = sgubt_ref[:, hd:hd + 1]
        cs = slice(hd * SGU_HEAD, (hd + 1) * SGU_HEAD)
        for c in range(tb // SGU_CHUNK):
            rs = slice(c * SGU_CHUNK, (c + 1) * SGU_CHUNK)
            s = jnp.dot(w_h, v[rs, cs], preferred_element_type=F32) + bias
            ysgu_ref[rs, cs] = (u[rs, cs] * s).astype(ysgu_ref.dtype)

    pbuf_ref[POOL_HALO:POOL_HALO + tb, :] = pz_ref[...]
    pg = poolw_ref.shape[1]
    t_glob = i * tb + lax.broadcasted_iota(jnp.int32, (tb, 1), 0)
    for gi, win in enumerate(POOL_WINDOWS):
        cs = slice(gi * pg, (gi + 1) * pg)
        z = pbuf_ref[POOL_HALO:POOL_HALO + tb, cs]
        acc = z
        for j in range(1, win):
            acc = acc + pbuf_ref[POOL_HALO - j:POOL_HALO - j + tb, cs]
        cnt = jnp.minimum(t_glob + 1, win).astype(F32)
        d = acc / cnt - z
        y = _bdot(d, poolw_ref[gi])
        ypool_ref[:, cs] = (y * pools_ref[:, cs]).astype(ypool_ref.dtype)
    pbuf_ref[0:POOL_HALO, :] = pbuf_ref[tb:tb + POOL_HALO, :]


def _local_mixers(proj, G, p):
    T = proj.shape[0]
    tb = _pick(T, (256, 128))
    n_sgu = G // SGU_HEAD
    col = lambda c: pl.BlockSpec((tb, G), lambda i, c=c: (i, c))
    full = lambda a: pl.BlockSpec(a.shape, lambda i, n=a.ndim: (0,) * n)
    convw = p['conv_w']
    lng = p['sgu_ln_g'].reshape(1, G)
    lnb = p['sgu_ln_b'].reshape(1, G)
    sguw = p['sgu_w']
    sgubt = p['sgu_b'].T
    poolw = p['pool_w'].astype(BF16)
    pools = p['pool_scale'].reshape(1, G)
    outs = pl.pallas_call(
        functools.partial(_local_mixers_kernel, tb=tb),
        grid=(T // tb,),
        in_specs=[col(0), col(1), col(2), col(6), col(7), col(8),
                  full(convw), full(lng), full(lnb), full(sguw), full(sgubt), full(poolw), full(pools)],
        out_specs=[pl.BlockSpec((tb, G), lambda i: (i, 0))] * 3,
        out_shape=[jax.ShapeDtypeStruct((T, G), BF16)] * 3,
        scratch_shapes=[pltpu.VMEM((tb + CONV_HALO, G), F32), pltpu.VMEM((tb + POOL_HALO, G), F32)],
        compiler_params=_cparams(("arbitrary",)),
        name="local_mixers",
    )(proj, proj, proj, proj, proj, proj, convw, lng, lnb, sguw, sgubt, poolw, pools)
    assert n_sgu == sguw.shape[0]
    return outs


def _each(fn, *lists):
    return [fn(*args) for args in zip(*lists)]


def _unit_lower_inverse(Ls, eye, row, col):
    blk16 = (row >> 4) == (col >> 4)
    blk32 = (row >> 5) == (col >> 5)
    P = [jnp.where(blk16, L, 0.0) for L in Ls]
    T = [eye + p for p in P]
    for _ in range(3):
        P = _each(_pdot, P, P)
        T = _each(lambda t, p: t + _pdot(t, p), T, P)
    for off_diag in (blk32 & jnp.logical_not(blk16), jnp.logical_not(blk32)):
        X = _each(lambda L, t: _pdot(jnp.where(off_diag, L, 0.0), t), Ls, T)
        T = _each(lambda t, x: t + _pdot(t, x), T, X)
    return T


def _block_diag(x):
    x = x.astype(BF16)
    first = lax.broadcasted_iota(jnp.int32, x.shape, 1) < RWKV_HEAD
    zero = jnp.zeros_like(x)
    return jnp.concatenate([jnp.where(first, x, zero), jnp.where(first, zero, x)], axis=0)


def _fold_diag(x):
    first = lax.broadcasted_iota(jnp.int32, (RWKV_HEAD, LANES), 1) < RWKV_HEAD
    return jnp.where(first, x[:RWKV_HEAD], x[RWKV_HEAD:])


def _pdot(a, b):
    return jnp.dot(a.astype(BF16), _block_diag(b), preferred_element_type=F32)


def _pdot_nt(a, b):
    return lax.dot_general(a.astype(BF16), _block_diag(b), (((1,), (1,)), ((), ())), preferred_element_type=F32)


def _pdot_tn(a, b):
    return _fold_diag(_bdot_tn(a, b))


def _bdot_tn(a, b):
    return lax.dot_general(a.astype(BF16), b.astype(BF16), (((0,), (0,)), ((), ())), preferred_element_type=F32)


def _wkv_chunk_operators(at, rt, bt, kt, bh, kh, v):
    C = at[0].shape[0]
    assert C == RWKV_HEAD
    row = lax.broadcasted_iota(jnp.int32, (C, LANES), 0)
    col = lax.broadcasted_iota(jnp.int32, (C, LANES), 1) & (RWKV_HEAD - 1)
    incl = col <= row
    strict = col < row
    eye = jnp.where(row == col, 1.0, 0.0).astype(F32)
    ar = _each(lambda x, y: jnp.concatenate([x, y.astype(BF16)], axis=0), at, rt)
    p_b = _each(_pdot_nt, ar, bt)
    p_k = _each(_pdot_nt, ar, kt)
    Lab = [jnp.where(strict, p[:C], 0.0) for p in p_b]
    Mak = [jnp.where(strict, p[:C], 0.0) for p in p_k]
    Mrb = [jnp.where(incl, p[C:], 0.0).astype(BF16) for p in p_b]
    Mrk = [jnp.where(incl, p[C:], 0.0) for p in p_k]
    MakV = _each(_pdot, Mak, v)
    MrkV = _each(_pdot, Mrk, v)
    KV = _each(_pdot_tn, v, kh)
    Tinv = [t.astype(BF16) for t in _unit_lower_inverse(Lab, eye, row, col)]
    W = [w.astype(BF16) for w in _each(_pdot, Tinv, at)]
    U = [u.astype(BF16) for u in _each(_pdot, Tinv, MakV)]
    Q = _each(lambda x, m, w: x + _pdot(m, w), rt, Mrb, W)
    y_add = _each(lambda m, u, mv: _pdot(m, u) + mv, Mrb, U, MrkV)
    m_state = _each(_pdot_tn, W, bh)
    s_add = _each(lambda u, b_, kv: _pdot_tn(u, b_) + kv, U, bh, KV)
    return Q, y_add, m_state, s_add


def _rwkv_kernel(*refs, tb, has_vres, n_cast):
    n_in = 21 if has_vres else 18
    n_out = 1 if has_vres else 2
    ins, refs = refs[:n_in], refs[n_in:]
    cast_in, refs = refs[:n_cast], refs[n_cast:]
    outs, refs = refs[:n_out], refs[n_out:]
    cast_out, refs = refs[:n_cast], refs[n_cast:]
    prev_ref, prevs_ref, S_ref, ybuf_ref = refs
    if has_vres:
        (r_ref, k_ref, v_ref, sm_ref, vfirst_ref,
         mur_ref, muk_ref, muv_ref, mus_ref,
         w0_ref, wup_ref, a0_ref, aup_ref, gup_ref, v0_ref, vup_ref,
         kk_ref, ka_ref, rk_ref, lng_ref, lnb_ref) = ins
        (y_ref,) = outs
    else:
        (r_ref, k_ref, v_ref, sm_ref,
         mur_ref, muk_ref, muv_ref, mus_ref,
         w0_ref, wup_ref, a0_ref, aup_ref, gup_ref,
         kk_ref, ka_ref, rk_ref, lng_ref, lnb_ref) = ins
        y_ref, vfirst_out_ref = outs
    t = pl.program_id(1)

    for src_ref, dst_ref in zip(cast_in, cast_out):
        dst_ref[...] = src_ref[...].astype(BF16)

    @pl.when(t == 0)
    def _():
        prev_ref[...] = jnp.zeros(prev_ref.shape, F32)
        prevs_ref[...] = jnp.zeros(prevs_ref.shape, F32)
        S_ref[...] = jnp.zeros(S_ref.shape, F32)

    def shift_mix(raw, prev_row, mu):
        first = lax.broadcasted_iota(jnp.int32, raw.shape, 0) == 0
        sh = jnp.where(first, prev_row, pltpu.roll(raw, 1, 0))
        return raw + (sh - raw) * mu

    r_raw, k_raw, v_raw, sm_raw = r_ref[...], k_ref[...], v_ref[...], sm_ref[...]
    r = shift_mix(r_raw, prev_ref[0, 7:8, :], mur_ref[...])
    k = shift_mix(k_raw, prev_ref[1, 7:8, :], muk_ref[...])
    v = shift_mix(v_raw, prev_ref[2, 7:8, :], muv_ref[...])
    sm = shift_mix(sm_raw, prevs_ref[7:8, :], mus_ref[...])
    prev_ref[0] = r_raw[tb - 8:tb, :]
    prev_ref[1] = k_raw[tb - 8:tb, :]
    prev_ref[2] = v_raw[tb - 8:tb, :]
    prevs_ref[...] = sm_raw[tb - 8:tb, :]

    dl, al, gl = wup_ref.shape[0], aup_ref.shape[0], gup_ref.shape[0]
    wd = sm[:, 0:dl]
    ad = sm[:, dl:dl + al]
    gd = sm[:, dl + al:dl + al + gl]
    w = -_softplus(-(w0_ref[...] + _bdot(jnp.tanh(wd), wup_ref[...]))) - 0.5
    lw = -jnp.exp(w)
    a = _sigmoid(a0_ref[...] + _bdot(ad, aup_ref[...]))
    g = _bdot(_sigmoid(gd), gup_ref[...])
    if has_vres:
        ml = vup_ref.shape[0]
        vd = sm[:, dl + al + gl:dl + al + gl + ml]
        v = v + (vfirst_ref[...] - v) * _sigmoid(v0_ref[...] + _bdot(vd, vup_ref[...]))
    else:
        vfirst_out_ref[...] = v

    li = lax.broadcasted_iota(jnp.int32, (LANES, LANES), 0) // RWKV_HEAD
    lj = lax.broadcasted_iota(jnp.int32, (LANES, LANES), 1) // RWKV_HEAD
    head_ones = jnp.where(li == lj, 1.0, 0.0).astype(BF16)
    width = y_ref.shape[1]
    n_pairs = width // LANES

    def split3(z):
        hi = z.astype(BF16)
        r1 = z - hi.astype(F32)
        mid = r1.astype(BF16)
        return hi, mid, (r1 - mid.astype(F32)).astype(BF16)

    def head_sum(z):
        parts = split3(z)
        return jnp.concatenate(
            [sum(jnp.dot(q[:, j * LANES:(j + 1) * LANES], head_ones, preferred_element_type=F32) for q in parts)
             for j in range(n_pairs)], axis=1)

    kk = k * kk_ref[...]
    kk = kk / jnp.maximum(jnp.sqrt(head_sum(kk * kk)), 1e-12)
    k = k * (1.0 + (a - 1.0) * ka_ref[...])

    C = WKV_CHUNK
    n_chunks = tb // C
    ti = lax.broadcasted_iota(jnp.int32, (tb, tb), 0)
    tj = lax.broadcasted_iota(jnp.int32, (tb, tb), 1)
    tri = jnp.where((tj <= ti) & ((ti // C) == (tj // C)), 1.0, 0.0).astype(BF16)
    lc = sum(jnp.dot(tri, q, preferred_element_type=F32) for q in split3(lw))
    lc_end = jnp.concatenate(
        [jnp.broadcast_to(lc[(c + 1) * C - 1:(c + 1) * C, :], (C, width)) for c in range(n_chunks)], axis=0)
    e_neg = jnp.exp(-lc)
    e_end = jnp.exp(lc_end - lc)
    at = (-kk * jnp.exp(lc - lw)).astype(BF16)
    rt = r * jnp.exp(lc)
    bt = (kk * a * e_neg).astype(BF16)
    kt = (k * e_neg).astype(BF16)
    bh = (kk * a * e_end).astype(BF16)
    kh = (k * e_end).astype(BF16)
    vb = v.astype(BF16)
    decay_end = jnp.exp(lc_end)

    tiles = [(c, j) for c in range(n_chunks) for j in range(n_pairs)]
    cut = lambda z: [z[c * C:(c + 1) * C, j * LANES:(j + 1) * LANES] for c, j in tiles]
    Q, y_add, m_state, s_add = _wkv_chunk_operators(*[cut(z) for z in (at, rt, bt, kt, bh, kh, vb)])
    S = [S_ref[j] for j in range(n_pairs)]
    for c in range(n_chunks):
        for j in range(n_pairs):
            i = c * n_pairs + j
            ls = slice(j * LANES, (j + 1) * LANES)
            ybuf_ref[c * C:(c + 1) * C, ls] = _pdot_nt(Q[i], S[j]) + y_add[i]
            S[j] = S[j] * decay_end[c * C:c * C + 1, ls] + _pdot(S[j], m_state[i]) + s_add[i]
    for j in range(n_pairs):
        S_ref[j] = S[j]

    y = ybuf_ref[...]
    inv_n = 1.0 / RWKV_HEAD
    mu = head_sum(y) * inv_n
    yc = y - mu
    var = head_sum(yc * yc) * inv_n
    y = yc * lax.rsqrt(var + RWKV_GN_EPS) * lng_ref[...] + lnb_ref[...]
    y = y + head_sum(r * k * rk_ref[...]) * v
    y_ref[...] = (y * g).astype(y_ref.dtype)


def _cast_plan(shape, n_steps):
    rows, cols = shape
    for ncb in (1, 2, 4, 8, 16):
        rb = n_steps // ncb
        if (n_steps % ncb == 0 and rows % rb == 0 and cols % ncb == 0
                and (rows // rb) % BF16_SUBLANES == 0 and (cols // ncb) % LANES == 0):
            return rb, ncb
    return None


def _rwkv_mixer(proj, G, small_w, small_off, p, v_first, to_cast):
    T = proj.shape[0]
    has_vres = v_first is not None
    tb = _pick(T, (256, 128, 64))
    bw = _pick(G, (RWKV_BLOCK_LANES, 2 * LANES, LANES))
    cb = G // bw
    dl, al, gl = p['decay_up'].shape[0], p['iclr_up'].shape[0], p['gate_up'].shape[0]
    ml = p['vres_up'].shape[0] if has_vres else 0
    mu = p['shift_mu']
    mu_r, mu_k, mu_v = (mu[q * G:(q + 1) * G].reshape(1, G) for q in range(3))
    mu_s = jnp.pad(mu[3 * G:], (0, small_w - (dl + al + gl + ml))).reshape(1, small_w)

    def colblk(first):
        return pl.BlockSpec((tb, bw), lambda q, t, f=first: (t, f + q))

    def vec(arr):
        return arr.reshape(1, G), pl.BlockSpec((1, bw), lambda q, t: (0, q))

    def up(arr):
        return arr.astype(BF16), pl.BlockSpec((arr.shape[0], bw), lambda q, t: (0, q))

    small_spec = pl.BlockSpec((tb, small_w), lambda q, t: (t, small_off // small_w))
    pair_spec = pl.BlockSpec((tb, bw), lambda q, t: (t, q))
    mu_spec = pl.BlockSpec((1, bw), lambda q, t: (0, q))
    args = [proj, proj, proj, proj]
    specs = [colblk(3 * cb), colblk(4 * cb), colblk(5 * cb), small_spec]
    if has_vres:
        args.append(v_first)
        specs.append(pair_spec)
    args += [mu_r, mu_k, mu_v, mu_s]
    specs += [mu_spec, mu_spec, mu_spec, pl.BlockSpec((1, small_w), lambda q, t: (0, 0))]
    names = ['decay_w0', 'decay_up', 'iclr_a0', 'iclr_up', 'gate_up']
    if has_vres:
        names += ['vres_v0', 'vres_up']
    names += ['k_k', 'k_a', 'r_k', 'lnx_g', 'lnx_b']
    for nm in names:
        arr, spec = up(p[nm]) if nm.endswith('_up') else vec(p[nm])
        args.append(arr)
        specs.append(spec)
    out_shape = [jax.ShapeDtypeStruct((T, G), BF16)]
    out_specs = [pair_spec]
    if not has_vres:
        out_shape.append(jax.ShapeDtypeStruct((T, G), F32))
        out_specs.append(pair_spec)
    n_t = T // tb
    n_steps = (G // bw) * n_t
    cast_names, cast_out = [], {}
    for nm, arr in to_cast.items():
        plan = _cast_plan(arr.shape, n_steps)
        if plan is None:
            cast_out[nm] = arr.astype(BF16)
            continue
        rb, ncb = plan
        spec = pl.BlockSpec((arr.shape[0] // rb, arr.shape[1] // ncb),
                            lambda q, t, ncb=ncb: ((q * n_t + t) // ncb, (q * n_t + t) % ncb))
        cast_names.append(nm)
        args.append(arr)
        specs.append(spec)
        out_shape.append(jax.ShapeDtypeStruct(arr.shape, BF16))
        out_specs.append(spec)
    n_main_out = 1 if has_vres else 2
    outs = pl.pallas_call(
        functools.partial(_rwkv_kernel, tb=tb, has_vres=has_vres, n_cast=len(cast_names)),
        grid=(G // bw, T // tb),
        in_specs=specs,
        out_specs=out_specs,
        out_shape=out_shape,
        scratch_shapes=[pltpu.VMEM((3, 8, bw), F32), pltpu.VMEM((8, small_w), F32),
                        pltpu.VMEM((bw // LANES, RWKV_HEAD, LANES), F32),
                        pltpu.VMEM((tb, bw), F32)],
        compiler_params=_cparams(("arbitrary", "arbitrary")),
        name="rwkv7_mixer",
    )(*args)
    cast_out.update(zip(cast_names, outs[n_main_out:]))
    return outs[0], (v_first if has_vres else outs[1]), cast_out


def _outproj_kernel(y0_ref, y1_ref, y2_ref, y3_ref, w_ref, x_ref, g_ref, b_ref, o_ref, ycat_ref, *, tn):
    G = y0_ref.shape[1]
    for j, y_ref in enumerate((y0_ref, y1_ref, y2_ref, y3_ref)):
        ycat_ref[:, j * G:(j + 1) * G] = y_ref[...]
    ycat = ycat_ref[...]
    for n in range(0, o_ref.shape[1], tn):
        o_ref[:, n:n + tn] = (DEEPNORM_ALPHA * x_ref[:, n:n + tn]
                              + jnp.dot(ycat, w_ref[:, n:n + tn], preferred_element_type=F32))
    o_ref[...] = _layer_norm(o_ref[...], g_ref[...], b_ref[...], LN_EPS)


def _outproj_ln(ys, w, x, g, b):
    T, D = x.shape
    G = ys[0].shape[1]
    tm = _pick(T, (256, 128))
    yspec = pl.BlockSpec((tm, G), lambda i: (i, 0))
    row = pl.BlockSpec((tm, D), lambda i: (i, 0))
    vec = pl.BlockSpec((1, D), lambda i: (0, 0))
    wspec = pl.BlockSpec(w.shape, lambda i: (0, 0), pipeline_mode=pl.Buffered(1))
    return pl.pallas_call(
        functools.partial(_outproj_kernel, tn=_pick(D, (512, 256, 128))),
        grid=(T // tm,),
        in_specs=[yspec] * 4 + [wspec, row, vec, vec],
        out_specs=row,
        out_shape=jax.ShapeDtypeStruct((T, D), F32),
        scratch_shapes=[pltpu.VMEM((tm, len(ys) * G), BF16)],
        compiler_params=_cparams(("arbitrary",)),
        name="outproj_ln",
    )(*ys, w, x, g.reshape(1, D), b.reshape(1, D))


def _ffn_kernel(x_ref, wg_ref, wu_ref, wd_ref, g_ref, b_ref, o_ref, xb_ref, *, tn):
    f = pl.program_id(1)
    nf = pl.num_programs(1)

    tm = o_ref.shape[0]
    tr = _pick(tm, (256, 128))

    def rows(c):
        return pl.ds(pl.multiple_of(c * tr, tr), tr)

    @pl.when(f == 0)
    def _():
        def body(c, carry):
            xr = x_ref[rows(c), :]
            xb_ref[rows(c), :] = xr.astype(BF16)
            o_ref[rows(c), :] = DEEPNORM_ALPHA * xr
            return carry
        lax.fori_loop(0, tm // tr, body, 0)

    xb = xb_ref[...]
    gate = jnp.dot(xb, wg_ref[...], preferred_element_type=F32)
    upv = jnp.dot(xb, wu_ref[...], preferred_element_type=F32)
    hid = (gate * _sigmoid(gate) * upv).astype(BF16)
    for n in range(0, o_ref.shape[1], tn):
        o_ref[:, n:n + tn] += jnp.dot(hid, wd_ref[:, n:n + tn], preferred_element_type=F32)

    @pl.when(f == nf - 1)
    def _():
        def body(c, carry):
            o_ref[rows(c), :] = _layer_norm(o_ref[rows(c), :], g_ref[...], b_ref[...], LN_EPS)
            return carry
        lax.fori_loop(0, tm // tr, body, 0)


def _ffn_ln(x, wg, wu, wd, g, b):
    T, D = x.shape
    F = wg.shape[1]
    tm = _pick(T, (1024, 512, 256, 128))
    tf = _pick(F, (256, 128))
    row = pl.BlockSpec((tm, D), lambda i, f: (i, 0), pipeline_mode=pl.Buffered(1))
    row_once = row
    vec = pl.BlockSpec((1, D), lambda i, f: (0, 0))
    return pl.pallas_call(
        functools.partial(_ffn_kernel, tn=_pick(D, (512, 256, 128))),
        grid=(T // tm, F // tf),
        in_specs=[row_once, pl.BlockSpec((D, tf), lambda i, f: (0, f)), pl.BlockSpec((D, tf), lambda i, f: (0, f)),
                  pl.BlockSpec((tf, D), lambda i, f: (f, 0)), vec, vec],
        out_specs=row,
        out_shape=jax.ShapeDtypeStruct((T, D), F32),
        scratch_shapes=[pltpu.VMEM((tm, D), BF16)],
        compiler_params=_cparams(("arbitrary", "arbitrary")),
        name="ffn_ln",
    )(x, wg, wu, wd, g.reshape(1, D), b.reshape(1, D))


def _hybrid_layer(x, v_first, p, w_in_t, next_w_in):
    T, D = x.shape
    G = D // 4
    n_small = w_in_t.shape[0] - 9 * G
    small_w = _round_up(n_small, 2 * LANES)
    proj = _inproj(x, w_in_t, 6 * G, n_small, 9 * G + small_w)
    y_conv, y_sgu, y_pool = _local_mixers(proj, G, p)
    later = {nm: p[nm] for nm in ('w_out', 'ffn_gate', 'ffn_up', 'ffn_down')}
    if next_w_in is not None:
        later['w_in_next_t'] = next_w_in.T
    y_rwkv, v_first, wb = _rwkv_mixer(proj, G, small_w, 9 * G, p, v_first, later)
    x = _outproj_ln((y_conv, y_rwkv, y_sgu, y_pool), wb['w_out'], x, p['ln_mix_g'], p['ln_mix_b'])
    x = _ffn_ln(x, wb['ffn_gate'], wb['ffn_up'], wb['ffn_down'], p['ln_ffn_g'], p['ln_ffn_b'])
    return x, v_first, wb.get('w_in_next_t')


_NAMES_0 = ('w_in', 'conv_w', 'shift_mu', 'decay_w0', 'decay_up', 'iclr_a0', 'iclr_up', 'gate_up',
            'k_k', 'k_a', 'r_k', 'lnx_g', 'lnx_b', 'sgu_ln_g', 'sgu_ln_b', 'sgu_w', 'sgu_b',
            'pool_w', 'pool_scale', 'w_out', 'ln_mix_g', 'ln_mix_b', 'ffn_gate', 'ffn_up', 'ffn_down',
            'ln_ffn_g', 'ln_ffn_b')
_NAMES_1 = _NAMES_0[:7] + ('vres_v0', 'vres_up') + _NAMES_0[7:]


def kernel(x, w_in_0, conv_w_0, shift_mu_0, decay_w0_0, decay_up_0, iclr_a0_0, iclr_up_0, gate_up_0, k_k_0, k_a_0, r_k_0, lnx_g_0, lnx_b_0, sgu_ln_g_0, sgu_ln_b_0, sgu_w_0, sgu_b_0, pool_w_0, pool_scale_0, w_out_0, ln_mix_g_0, ln_mix_b_0, ffn_gate_0, ffn_up_0, ffn_down_0, ln_ffn_g_0, ln_ffn_b_0, w_in_1, conv_w_1, shift_mu_1, decay_w0_1, decay_up_1, iclr_a0_1, iclr_up_1, vres_v0_1, vres_up_1, gate_up_1, k_k_1, k_a_1, r_k_1, lnx_g_1, lnx_b_1, sgu_ln_g_1, sgu_ln_b_1, sgu_w_1, sgu_b_1, pool_w_1, pool_scale_1, w_out_1, ln_mix_g_1, ln_mix_b_1, ffn_gate_1, ffn_up_1, ffn_down_1, ln_ffn_g_1, ln_ffn_b_1):
    p0 = dict(zip(_NAMES_0, (w_in_0, conv_w_0, shift_mu_0, decay_w0_0, decay_up_0, iclr_a0_0, iclr_up_0, gate_up_0, k_k_0, k_a_0, r_k_0, lnx_g_0, lnx_b_0, sgu_ln_g_0, sgu_ln_b_0, sgu_w_0, sgu_b_0, pool_w_0, pool_scale_0, w_out_0, ln_mix_g_0, ln_mix_b_0, ffn_gate_0, ffn_up_0, ffn_down_0, ln_ffn_g_0, ln_ffn_b_0)))
    p1 = dict(zip(_NAMES_1, (w_in_1, conv_w_1, shift_mu_1, decay_w0_1, decay_up_1, iclr_a0_1, iclr_up_1, vres_v0_1, vres_up_1, gate_up_1, k_k_1, k_a_1, r_k_1, lnx_g_1, lnx_b_1, sgu_ln_g_1, sgu_ln_b_1, sgu_w_1, sgu_b_1, pool_w_1, pool_scale_1, w_out_1, ln_mix_g_1, ln_mix_b_1, ffn_gate_1, ffn_up_1, ffn_down_1, ln_ffn_g_1, ln_ffn_b_1)))
    B, T, D = x.shape
    assert B == 1
    h = x.reshape(T, D)
    h, v_first, w_in_t = _hybrid_layer(h, None, p0, w_in_0.T.astype(BF16), w_in_1)
    h, _, _ = _hybrid_layer(h, v_first, p1, w_in_t, None)
    return h.reshape(B, T, D)
```

```python
import functools

import jax
import jax.numpy as jnp
from jax import lax
from jax.experimental import pallas as pl
from jax.experimental.pallas import tpu as pltpu

LANES = 128
BF16_SUBLANES = 16
RWKV_HEAD = 64
SGU_CHUNK = 128
SGU_HEAD = 128
CONV_W = 3
POOL_WINDOWS = (2, 4, 8, 16)
POOL_HALO = 16
CONV_HALO = 8
LN_EPS = 1e-5
RWKV_GN_EPS = 64e-5
WKV_CHUNK = 64
RWKV_BLOCK_LANES = 1024
DEPTH = 2
DEEPNORM_ALPHA = (2 * DEPTH) ** 0.25
VMEM_LIMIT_BYTES = 60 * 1024 * 1024

F32 = jnp.float32
BF16 = jnp.bfloat16


def _cparams(sem):
    return pltpu.CompilerParams(dimension_semantics=sem, vmem_limit_bytes=VMEM_LIMIT_BYTES)


def _pick(n, prefs):
    for p in prefs:
        if n % p == 0:
            return p
    return n


def _round_up(n, m):
    return (n + m - 1) // m * m


def _bdot(a, b):
    return jnp.dot(a.astype(BF16), b.astype(BF16), preferred_element_type=F32)


def _layer_norm(x, g, b, eps):
    mu = jnp.mean(x, axis=-1, keepdims=True)
    xc = x - mu
    var = jnp.mean(xc * xc, axis=-1, keepdims=True)
    return xc * lax.rsqrt(var + eps) * g + b


def _sigmoid(x):
    return 1.0 / (1.0 + jnp.exp(-x))


def _gelu_tanh(x):
    c = 0.7978845608028654
    return 0.5 * x * (1.0 + jnp.tanh(c * (x + 0.044715 * (x * x * x))))


def _softplus(x):
    return jnp.maximum(x, 0.0) + jnp.log(1.0 + jnp.exp(-jnp.abs(x)))


def _inproj_kernel(x_ref, wt_ref, o_ref, xb_ref):
    @pl.when(pl.program_id(1) == 0)
    def _():
        xb_ref[...] = x_ref[...].astype(BF16)

    o_ref[...] = lax.dot_general(xb_ref[...], wt_ref[...], (((1,), (1,)), ((), ())), preferred_element_type=F32)


def _inproj(x, wt, head, n_small, width):
    T, D = x.shape
    tm = _pick(T, (1024, 512, 256, 128))
    tail = width - (wt.shape[0] - n_small)
    tn = next(c for c in (512, 256, 128) if head % c == 0 and width % c == 0 and tail == c)
    n_head = head // tn
    n_blocks = width // tn

    assert n_small % BF16_SUBLANES == 0

    def row_start(i, j):
        moved = head + n_small + (j - n_head) * tn
        start = jnp.where(j < n_head, j * tn, jnp.where(j < n_blocks - 1, moved, head))
        return pl.multiple_of(start, BF16_SUBLANES), 0

    return pl.pallas_call(
        _inproj_kernel,
        grid=(T // tm, n_blocks),
        in_specs=[pl.BlockSpec((tm, D), lambda i, j: (i, 0)),
                  pl.BlockSpec((pl.Element(tn), pl.Element(D)), row_start)],
        out_specs=pl.BlockSpec((tm, tn), lambda i, j: (i, j)),
        out_shape=jax.ShapeDtypeStruct((T, width), F32),
        scratch_shapes=[pltpu.VMEM((tm, D), BF16)],
        compiler_params=_cparams(("arbitrary", "arbitrary")),
        name="inproj",
    )(x, wt)


def _local_mixers_kernel(h_ref, bg_ref, cg_ref, su_ref, sv_ref, pz_ref,
                         convw_ref, lng_ref, lnb_ref, sguw_ref, sgubt_ref, poolw_ref, pools_ref,
                         yconv_ref, ysgu_ref, ypool_ref,
                         zbuf_ref, pbuf_ref, *, tb):
    i = pl.program_id(0)

    @pl.when(i == 0)
    def _():
        zbuf_ref[0:CONV_HALO, :] = jnp.zeros((CONV_HALO, zbuf_ref.shape[1]), F32)
        pbuf_ref[0:POOL_HALO, :] = jnp.zeros((POOL_HALO, pbuf_ref.shape[1]), F32)

    zbuf_ref[CONV_HALO:CONV_HALO + tb, :] = cg_ref[...] * h_ref[...]
    conv = zbuf_ref[CONV_HALO - (CONV_W - 1):CONV_HALO - (CONV_W - 1) + tb, :] * convw_ref[0:1, :]
    for j in range(1, CONV_W):
        off = CONV_HALO - (CONV_W - 1 - j)
        conv = conv + zbuf_ref[off:off + tb, :] * convw_ref[j:j + 1, :]
    yconv_ref[...] = (bg_ref[...] * conv).astype(yconv_ref.dtype)
    zbuf_ref[0:CONV_HALO, :] = zbuf_ref[tb:tb + CONV_HALO, :]

    u = _gelu_tanh(su_ref[...])
    v = _layer_norm(_gelu_tanh(sv_ref[...]), lng_ref[...], lnb_ref[...], LN_EPS).astype(BF16)
    n_heads = sguw_ref.shape[0]
    row = lax.broadcasted_iota(jnp.int32, (SGU_CHUNK, SGU_CHUNK), 0)
    col = lax.broadcasted_iota(jnp.int32, (SGU_CHUNK, SGU_CHUNK), 1)
    causal = col <= row
    for hd in range(n_heads):
        w_h = jnp.where(causal, sguw_ref[hd], 0.0).astype(BF16)
        bias = sgubt_ref[:, hd:hd + 1]
        cs = slice(hd * SGU_HEAD, (hd + 1) * SGU_HEAD)
        for c in range(tb // SGU_CHUNK):
            rs = slice(c * SGU_CHUNK, (c + 1) * SGU_CHUNK)
            s = jnp.dot(w_h, v[rs, cs], preferred_element_type=F32) + bias
            ysgu_ref[rs, cs] = (u[rs, cs] * s).astype(ysgu_ref.dtype)

    pbuf_ref[POOL_HALO:POOL_HALO + tb, :] = pz_ref[...]
    pg = poolw_ref.shape[1]
    t_glob = i * tb + lax.broadcasted_iota(jnp.int32, (tb, 1), 0)
    for gi, win in enumerate(POOL_WINDOWS):
        cs = slice(gi * pg, (gi + 1) * pg)
        z = pbuf_ref[POOL_HALO:POOL_HALO + tb, cs]
        acc = z
        for j in range(1, win):
            acc = acc + pbuf_ref[POOL_HALO - j:POOL_HALO - j + tb, cs]
        cnt = jnp.minimum(t_glob + 1, win).astype(F32)
        d = acc / cnt - z
        y = _bdot(d, poolw_ref[gi])
        ypool_ref[:, cs] = (y * pools_ref[:, cs]).astype(ypool_ref.dtype)
    pbuf_ref[0:POOL_HALO, :] = pbuf_ref[tb:tb + POOL_HALO, :]


def _local_mixers(proj, G, p):
    T = proj.shape[0]
    tb = _pick(T, (256, 128))
    n_sgu = G // SGU_HEAD
    col = lambda c: pl.BlockSpec((tb, G), lambda i, c=c: (i, c))
    full = lambda a: pl.BlockSpec(a.shape, lambda i, n=a.ndim: (0,) * n)
    convw = p['conv_w']
    lng = p['sgu_ln_g'].reshape(1, G)
    lnb = p['sgu_ln_b'].reshape(1, G)
    sguw = p['sgu_w']
    sgubt = p['sgu_b'].T
    poolw = p['pool_w'].astype(BF16)
    pools = p['pool_scale'].reshape(1, G)
    outs = pl.pallas_call(
        functools.partial(_local_mixers_kernel, tb=tb),
        grid=(T // tb,),
        in_specs=[col(0), col(1), col(2), col(6), col(7), col(8),
                  full(convw), full(lng), full(lnb), full(sguw), full(sgubt), full(poolw), full(pools)],
        out_specs=[pl.BlockSpec((tb, G), lambda i: (i, 0))] * 3,
        out_shape=[jax.ShapeDtypeStruct((T, G), BF16)] * 3,
        scratch_shapes=[pltpu.VMEM((tb + CONV_HALO, G), F32), pltpu.VMEM((tb + POOL_HALO, G), F32)],
        compiler_params=_cparams(("arbitrary",)),
        name="local_mixers",
    )(proj, proj, proj, proj, proj, proj, convw, lng, lnb, sguw, sgubt, poolw, pools)
    assert n_sgu == sguw.shape[0]
    return outs


def _each(fn, *lists):
    return [fn(*args) for args in zip(*lists)]


def _unit_lower_inverse(Ls, eye, row, col):
    blk16 = (row >> 4) == (col >> 4)
    blk32 = (row >> 5) == (col >> 5)
    P = [jnp.where(blk16, L, 0.0) for L in Ls]
    T = [eye + p for p in P]
    for _ in range(3):
        P = _each(_pdot, P, P)
        T = _each(lambda t, p: t + _pdot(t, p), T, P)
    for off_diag in (blk32 & jnp.logical_not(blk16), jnp.logical_not(blk32)):
        X = _each(lambda L, t: _pdot(jnp.where(off_diag, L, 0.0), t), Ls, T)
        T = _each(lambda t, x: t + _pdot(t, x), T, X)
    return T


def _block_diag(x):
    x = x.astype(BF16)
    first = lax.broadcasted_iota(jnp.int32, x.shape, 1) < RWKV_HEAD
    zero = jnp.zeros_like(x)
    return jnp.concatenate([jnp.where(first, x, zero), jnp.where(first, zero, x)], axis=0)


def _fold_diag(x):
    first = lax.broadcasted_iota(jnp.int32, (RWKV_HEAD, LANES), 1) < RWKV_HEAD
    return jnp.where(first, x[:RWKV_HEAD], x[RWKV_HEAD:])


def _pdot(a, b):
    return jnp.dot(a.astype(BF16), _block_diag(b), preferred_element_type=F32)


def _pdot_nt(a, b):
    return lax.dot_general(a.astype(BF16), _block_diag(b), (((1,), (1,)), ((), ())), preferred_element_type=F32)


def _pdot_tn(a, b):
    return _fold_diag(_bdot_tn(a, b))


def _bdot_tn(a, b):
    return lax.dot_general(a.astype(BF16), b.astype(BF16), (((0,), (0,)), ((), ())), preferred_element_type=F32)


def _wkv_chunk_operators(at, rt, bt, kt, bh, kh, v):
    C = at[0].shape[0]
    assert C == RWKV_HEAD
    row = lax.broadcasted_iota(jnp.int32, (C, LANES), 0)
    col = lax.broadcasted_iota(jnp.int32, (C, LANES), 1) & (RWKV_HEAD - 1)
    incl = col <= row
    strict = col < row
    eye = jnp.where(row == col, 1.0, 0.0).astype(F32)
    ar = _each(lambda x, y: jnp.concatenate([x, y.astype(BF16)], axis=0), at, rt)
    p_b = _each(_pdot_nt, ar, bt)
    p_k = _each(_pdot_nt, ar, kt)
    Lab = [jnp.where(strict, p[:C], 0.0) for p in p_b]
    Mak = [jnp.where(strict, p[:C], 0.0) for p in p_k]
    Mrb = [jnp.where(incl, p[C:], 0.0).astype(BF16) for p in p_b]
    Mrk = [jnp.where(incl, p[C:], 0.0) for p in p_k]
    MakV = _each(_pdot, Mak, v)
    MrkV = _each(_pdot, Mrk, v)
    KV = _each(_pdot_tn, v, kh)
    Tinv = [t.astype(BF16) for t in _unit_lower_inverse(Lab, eye, row, col)]
    W = [w.astype(BF16) for w in _each(_pdot, Tinv, at)]
    U = [u.astype(BF16) for u in _each(_pdot, Tinv, MakV)]
    Q = _each(lambda x, m, w: x + _pdot(m, w), rt, Mrb, W)
    y_add = _each(lambda m, u, mv: _pdot(m, u) + mv, Mrb, U, MrkV)
    m_state = _each(_pdot_tn, W, bh)
    s_add = _each(lambda u, b_, kv: _pdot_tn(u, b_) + kv, U, bh, KV)
    return Q, y_add, m_state, s_add


def _rwkv_kernel(*refs, tb, has_vres, n_cast):
    n_in = 21 if has_vres else 18
    n_out = 1 if has_vres else 2
    ins, refs = refs[:n_in], refs[n_in:]
    cast_in, refs = refs[:n_cast], refs[n_cast:]
    outs, refs = refs[:n_out], refs[n_out:]
    cast_out, refs = refs[:n_cast], refs[n_cast:]
    prev_ref, prevs_ref, S_ref, ybuf_ref = refs
    if has_vres:
        (r_ref, k_ref, v_ref, sm_ref, vfirst_ref,
         mur_ref, muk_ref, muv_ref, mus_ref,
         w0_ref, wup_ref, a0_ref, aup_ref, gup_ref, v0_ref, vup_ref,
         kk_ref, ka_ref, rk_ref, lng_ref, lnb_ref) = ins
        (y_ref,) = outs
    else:
        (r_ref, k_ref, v_ref, sm_ref,
         mur_ref, muk_ref, muv_ref, mus_ref,
         w0_ref, wup_ref, a0_ref, aup_ref, gup_ref,
         kk_ref, ka_ref, rk_ref, lng_ref, lnb_ref) = ins
        y_ref, vfirst_out_ref = outs
    t = pl.program_id(1)

    for src_ref, dst_ref in zip(cast_in, cast_out):
        dst_ref[...] = src_ref[...].astype(BF16)

    @pl.when(t == 0)
    def _():
        prev_ref[...] = jnp.zeros(prev_ref.shape, F32)
        prevs_ref[...] = jnp.zeros(prevs_ref.shape, F32)
        S_ref[...] = jnp.zeros(S_ref.shape, F32)

    def shift_mix(raw, prev_row, mu):
        first = lax.broadcasted_iota(jnp.int32, raw.shape, 0) == 0
        sh = jnp.where(first, prev_row, pltpu.roll(raw, 1, 0))
        return raw + (sh - raw) * mu

    r_raw, k_raw, v_raw, sm_raw = r_ref[...], k_ref[...], v_ref[...], sm_ref[...]
    r = shift_mix(r_raw, prev_ref[0, 7:8, :], mur_ref[...])
    k = shift_mix(k_raw, prev_ref[1, 7:8, :], muk_ref[...])
    v = shift_mix(v_raw, prev_ref[2, 7:8, :], muv_ref[...])
    sm = shift_mix(sm_raw, prevs_ref[7:8, :], mus_ref[...])
    prev_ref[0] = r_raw[tb - 8:tb, :]
    prev_ref[1] = k_raw[tb - 8:tb, :]
    prev_ref[2] = v_raw[tb - 8:tb, :]
    prevs_ref[...] = sm_raw[tb - 8:tb, :]

    dl, al, gl = wup_ref.shape[0], aup_ref.shape[0], gup_ref.shape[0]
    wd = sm[:, 0:dl]
    ad = sm[:, dl:dl + al]
    gd = sm[:, dl + al:dl + al + gl]
    w = -_softplus(-(w0_ref[...] + _bdot(jnp.tanh(wd), wup_ref[...]))) - 0.5
    lw = -jnp.exp(w)
    a = _sigmoid(a0_ref[...] + _bdot(ad, aup_ref[...]))
    g = _bdot(_sigmoid(gd), gup_ref[...])
    if has_vres:
        ml = vup_ref.shape[0]
        vd = sm[:, dl + al + gl:dl + al + gl + ml]
        v = v + (vfirst_ref[...] - v) * _sigmoid(v0_ref[...] + _bdot(vd, vup_ref[...]))
    else:
        vfirst_out_ref[...] = v

    li = lax.broadcasted_iota(jnp.int32, (LANES, LANES), 0) // RWKV_HEAD
    lj = lax.broadcasted_iota(jnp.int32, (LANES, LANES), 1) // RWKV_HEAD
    head_ones = jnp.where(li == lj, 1.0, 0.0).astype(BF16)
    width = y_ref.shape[1]
    n_pairs = width // LANES

    def split3(z):
        hi = z.astype(BF16)
        r1 = z - hi.astype(F32)
        mid = r1.astype(BF16)
        return hi, mid, (r1 - mid.astype(F32)).astype(BF16)

    def head_sum(z):
        parts = split3(z)
        return jnp.concatenate(
            [sum(jnp.dot(q[:, j * LANES:(j + 1) * LANES], head_ones, preferred_element_type=F32) for q in parts)
             for j in range(n_pairs)], axis=1)

    kk = k * kk_ref[...]
    kk = kk / jnp.maximum(jnp.sqrt(head_sum(kk * kk)), 1e-12)
    k = k * (1.0 + (a - 1.0) * ka_ref[...])

    C = WKV_CHUNK
    n_chunks = tb // C
    ti = lax.broadcasted_iota(jnp.int32, (tb, tb), 0)
    tj = lax.broadcasted_iota(jnp.int32, (tb, tb), 1)
    tri = jnp.where((tj <= ti) & ((ti // C) == (tj // C)), 1.0, 0.0).astype(BF16)
    lc = sum(jnp.dot(tri, q, preferred_element_type=F32) for q in split3(lw))
    lc_end = jnp.concatenate(
        [jnp.broadcast_to(lc[(c + 1) * C - 1:(c + 1) * C, :], (C, width)) for c in range(n_chunks)], axis=0)
    e_neg = jnp.exp(-lc)
    e_end = jnp.exp(lc_end - lc)
    at = (-kk * jnp.exp(lc - lw)).astype(BF16)
    rt = r * jnp.exp(lc)
    bt = (kk * a * e_neg).astype(BF16)
    kt = (k * e_neg).astype(BF16)
    bh = (kk * a * e_end).astype(BF16)
    kh = (k * e_end).astype(BF16)
    vb = v.astype(BF16)
    decay_end = jnp.exp(lc_end)

    tiles = [(c, j) for c in range(n_chunks) for j in range(n_pairs)]
    cut = lambda z: [z[c * C:(c + 1) * C, j * LANES:(j + 1) * LANES] for c, j in tiles]
    Q, y_add, m_state, s_add = _wkv_chunk_operators(*[cut(z) for z in (at, rt, bt, kt, bh, kh, vb)])
    S = [S_ref[j] for j in range(n_pairs)]
    for c in range(n_chunks):
        for j in range(n_pairs):
            i = c * n_pairs + j
            ls = slice(j * LANES, (j + 1) * LANES)
            ybuf_ref[c * C:(c + 1) * C, ls] = _pdot_nt(Q[i], S[j]) + y_add[i]
            S[j] = S[j] * decay_end[c * C:c * C + 1, ls] + _pdot(S[j], m_state[i]) + s_add[i]
    for j in range(n_pairs):
        S_ref[j] = S[j]

    y = ybuf_ref[...]
    inv_n = 1.0 / RWKV_HEAD
    mu = head_sum(y) * inv_n
    yc = y - mu
    var = head_sum(yc * yc) * inv_n
    y = yc * lax.rsqrt(var + RWKV_GN_EPS) * lng_ref[...] + lnb_ref[...]
    y = y + head_sum(r * k * rk_ref[...]) * v
    y_ref[...] = (y * g).astype(y_ref.dtype)


def _cast_plan(shape, n_steps):
    rows, cols = shape
    for ncb in (1, 2, 4, 8, 16):
        rb = n_steps // ncb
        if (n_steps % ncb == 0 and rows % rb == 0 and cols % ncb == 0
                and (rows // rb) % BF16_SUBLANES == 0 and (cols // ncb) % LANES == 0):
            return rb, ncb
    return None


def _rwkv_mixer(proj, G, small_w, small_off, p, v_first, to_cast):
    T = proj.shape[0]
    has_vres = v_first is not None
    tb = _pick(T, (256, 128, 64))
    bw = _pick(G, (RWKV_BLOCK_LANES, 2 * LANES, LANES))
    cb = G // bw
    dl, al, gl = p['decay_up'].shape[0], p['iclr_up'].shape[0], p['gate_up'].shape[0]
    ml = p['vres_up'].shape[0] if has_vres else 0
    mu = p['shift_mu']
    mu_r, mu_k, mu_v = (mu[q * G:(q + 1) * G].reshape(1, G) for q in range(3))
    mu_s = jnp.pad(mu[3 * G:], (0, small_w - (dl + al + gl + ml))).reshape(1, small_w)

    def colblk(first):
        return pl.BlockSpec((tb, bw), lambda q, t, f=first: (t, f + q))

    def vec(arr):
        return arr.reshape(1, G), pl.BlockSpec((1, bw), lambda q, t: (0, q))

    def up(arr):
        return arr.astype(BF16), pl.BlockSpec((arr.shape[0], bw), lambda q, t: (0, q))

    small_spec = pl.BlockSpec((tb, small_w), lambda q, t: (t, small_off // small_w))
    pair_spec = pl.BlockSpec((tb, bw), lambda q, t: (t, q))
    mu_spec = pl.BlockSpec((1, bw), lambda q, t: (0, q))
    args = [proj, proj, proj, proj]
    specs = [colblk(3 * cb), colblk(4 * cb), colblk(5 * cb), small_spec]
    if has_vres:
        args.append(v_first)
        specs.append(pair_spec)
    args += [mu_r, mu_k, mu_v, mu_s]
    specs += [mu_spec, mu_spec, mu_spec, pl.BlockSpec((1, small_w), lambda q, t: (0, 0))]
    names = ['decay_w0', 'decay_up', 'iclr_a0', 'iclr_up', 'gate_up']
    if has_vres:
        names += ['vres_v0', 'vres_up']
    names += ['k_k', 'k_a', 'r_k', 'lnx_g', 'lnx_b']
    for nm in names:
        arr, spec = up(p[nm]) if nm.endswith('_up') else vec(p[nm])
        args.append(arr)
        specs.append(spec)
    out_shape = [jax.ShapeDtypeStruct((T, G), BF16)]
    out_specs = [pair_spec]
    if not has_vres:
        out_shape.append(jax.ShapeDtypeStruct((T, G), F32))
        out_specs.append(pair_spec)
    n_t = T // tb
    n_steps = (G // bw) * n_t
    cast_names, cast_out = [], {}
    for nm, arr in to_cast.items():
        plan = _cast_plan(arr.shape, n_steps)
        if plan is None:
            cast_out[nm] = arr.astype(BF16)
            continue
        rb, ncb = plan
        spec = pl.BlockSpec((arr.shape[0] // rb, arr.shape[1] // ncb),
                            lambda q, t, ncb=ncb: ((q * n_t + t) // ncb, (q * n_t + t) % ncb))
        cast_names.append(nm)
        args.append(arr)
        specs.append(spec)
        out_shape.append(jax.ShapeDtypeStruct(arr.shape, BF16))
        out_specs.append(spec)
    n_main_out = 1 if has_vres else 2
    outs = pl.pallas_call(
        functools.partial(_rwkv_kernel, tb=tb, has_vres=has_vres, n_cast=len(cast_names)),
        grid=(G // bw, T // tb),
        in_specs=specs,
        out_specs=out_specs,
        out_shape=out_shape,
        scratch_shapes=[pltpu.VMEM((3, 8, bw), F32), pltpu.VMEM((8, small_w), F32),
                        pltpu.VMEM((bw // LANES, RWKV_HEAD, LANES), F32),
                        pltpu.VMEM((tb, bw), F32)],
        compiler_params=_cparams(("arbitrary", "arbitrary")),
        name="rwkv7_mixer",
    )(*args)
    cast_out.update(zip(cast_names, outs[n_main_out:]))
    return outs[0], (v_first if has_vres else outs[1]), cast_out


def _outproj_kernel(y0_ref, y1_ref, y2_ref, y3_ref, w_ref, x_ref, g_ref, b_ref, o_ref, ycat_ref, *, tn):
    G = y0_ref.shape[1]
    for j, y_ref in enumerate((y0_ref, y1_ref, y2_ref, y3_ref)):
        ycat_ref[:, j * G:(j + 1) * G] = y_ref[...]
    ycat = ycat_ref[...]
    for n in range(0, o_ref.shape[1], tn):
        o_ref[:, n:n + tn] = (DEEPNORM_ALPHA * x_ref[:, n:n + tn]
                              + jnp.dot(ycat, w_ref[:, n:n + tn], preferred_element_type=F32))
    o_ref[...] = _layer_norm(o_ref[...], g_ref[...], b_ref[...], LN_EPS)


def _outproj_ln(ys, w, x, g, b):
    T, D = x.shape
    G = ys[0].shape[1]
    tm = _pick(T, (256, 128))
    yspec = pl.BlockSpec((tm, G), lambda i: (i, 0))
    row = pl.BlockSpec((tm, D), lambda i: (i, 0))
    vec = pl.BlockSpec((1, D), lambda i: (0, 0))
    wspec = pl.BlockSpec(w.shape, lambda i: (0, 0), pipeline_mode=pl.Buffered(1))
    return pl.pallas_call(
        functools.partial(_outproj_kernel, tn=_pick(D, (512, 256, 128))),
        grid=(T // tm,),
        in_specs=[yspec] * 4 + [wspec, row, vec, vec],
        out_specs=row,
        out_shape=jax.ShapeDtypeStruct((T, D), F32),
        scratch_shapes=[pltpu.VMEM((tm, len(ys) * G), BF16)],
        compiler_params=_cparams(("arbitrary",)),
        name="outproj_ln",
    )(*ys, w, x, g.reshape(1, D), b.reshape(1, D))


def _ffn_kernel(x_hbm, wg_ref, wu_ref, wd_ref, g_ref, b_ref, o_hbm, acc_ref, xb_ref, in_sem, out_sem, *, tn):
    i, ni = pl.program_id(0), pl.num_programs(0)
    f, nf = pl.program_id(1), pl.num_programs(1)
    tm = xb_ref.shape[0]
    tr = _pick(tm, (256, 128))
    slot = i % 2

    def rows(c):
        return pl.ds(pl.multiple_of(c * tr, tr), tr)

    def x_copy(blk, s):
        return pltpu.make_async_copy(x_hbm.at[pl.ds(blk * tm, tm), :], acc_ref.at[s], in_sem.at[0])

    def out_copy(blk, s):
        return pltpu.make_async_copy(acc_ref.at[s], o_hbm.at[pl.ds(blk * tm, tm), :], out_sem.at[s])

    @pl.when(f == 0)
    def _():
        @pl.when(i == 0)
        def _():
            x_copy(0, 0).start()

        x_copy(i, slot).wait()

        def body(c, carry):
            xr = acc_ref[slot, rows(c), :]
            xb_ref[rows(c), :] = xr.astype(BF16)
            acc_ref[slot, rows(c), :] = DEEPNORM_ALPHA * xr
            return carry
        lax.fori_loop(0, tm // tr, body, 0)

    xb = xb_ref[...]
    gate = jnp.dot(xb, wg_ref[...], preferred_element_type=F32)
    upv = jnp.dot(xb, wu_ref[...], preferred_element_type=F32)
    hid = (gate * _sigmoid(gate) * upv).astype(BF16)
    for n in range(0, acc_ref.shape[2], tn):
        acc_ref[slot, :, n:n + tn] += jnp.dot(hid, wd_ref[:, n:n + tn], preferred_element_type=F32)

    @pl.when(f == jnp.maximum(nf - 2, 0))
    def _():
        @pl.when(i >= 1)
        def _():
            out_copy(i - 1, 1 - slot).wait()

        @pl.when(i + 1 < ni)
        def _():
            x_copy(i + 1, 1 - slot).start()

    @pl.when(f == nf - 1)
    def _():
        def body(c, carry):
            acc_ref[slot, rows(c), :] = _layer_norm(acc_ref[slot, rows(c), :], g_ref[...], b_ref[...], LN_EPS)
            return carry
        lax.fori_loop(0, tm // tr, body, 0)
        out_copy(i, slot).start()

        @pl.when(i == ni - 1)
        def _():
            out_copy(i, slot).wait()


def _ffn_ln(x, wg, wu, wd, g, b):
    T, D = x.shape
    F = wg.shape[1]
    tm = _pick(T, (1024, 512, 256, 128))
    tf = _pick(F, (256, 128))
    vec = pl.BlockSpec((1, D), lambda i, f: (0, 0))
    return pl.pallas_call(
        functools.partial(_ffn_kernel, tn=_pick(D, (512, 256, 128))),
        grid=(T // tm, F // tf),
        in_specs=[pl.BlockSpec(memory_space=pl.ANY),
                  pl.BlockSpec((D, tf), lambda i, f: (0, f)), pl.BlockSpec((D, tf), lambda i, f: (0, f)),
                  pl.BlockSpec((tf, D), lambda i, f: (f, 0)), vec, vec],
        out_specs=pl.BlockSpec(memory_space=pl.ANY),
        out_shape=jax.ShapeDtypeStruct((T, D), F32),
        scratch_shapes=[pltpu.VMEM((2, tm, D), F32), pltpu.VMEM((tm, D), BF16),
                        pltpu.SemaphoreType.DMA((1,)), pltpu.SemaphoreType.DMA((2,))],
        compiler_params=_cparams(("arbitrary", "arbitrary")),
        name="ffn_ln",
    )(x, wg, wu, wd, g.reshape(1, D), b.reshape(1, D))


def _hybrid_layer(x, v_first, p, w_in_t, next_w_in):
    T, D = x.shape
    G = D // 4
    n_small = w_in_t.shape[0] - 9 * G
    small_w = _round_up(n_small, 2 * LANES)
    proj = _inproj(x, w_in_t, 6 * G, n_small, 9 * G + small_w)
    y_conv, y_sgu, y_pool = _local_mixers(proj, G, p)
    later = {nm: p[nm] for nm in ('w_out', 'ffn_gate', 'ffn_up', 'ffn_down')}
    if next_w_in is not None:
        later['w_in_next_t'] = next_w_in.T
    y_rwkv, v_first, wb = _rwkv_mixer(proj, G, small_w, 9 * G, p, v_first, later)
    x = _outproj_ln((y_conv, y_rwkv, y_sgu, y_pool), wb['w_out'], x, p['ln_mix_g'], p['ln_mix_b'])
    x = _ffn_ln(x, wb['ffn_gate'], wb['ffn_up'], wb['ffn_down'], p['ln_ffn_g'], p['ln_ffn_b'])
    return x, v_first, wb.get('w_in_next_t')


_NAMES_0 = ('w_in', 'conv_w', 'shift_mu', 'decay_w0', 'decay_up', 'iclr_a0', 'iclr_up', 'gate_up',
            'k_k', 'k_a', 'r_k', 'lnx_g', 'lnx_b', 'sgu_ln_g', 'sgu_ln_b', 'sgu_w', 'sgu_b',
            'pool_w', 'pool_scale', 'w_out', 'ln_mix_g', 'ln_mix_b', 'ffn_gate', 'ffn_up', 'ffn_down',
            'ln_ffn_g', 'ln_ffn_b')
_NAMES_1 = _NAMES_0[:7] + ('vres_v0', 'vres_up') + _NAMES_0[7:]


def kernel(x, w_in_0, conv_w_0, shift_mu_0, decay_w0_0, decay_up_0, iclr_a0_0, iclr_up_0, gate_up_0, k_k_0, k_a_0, r_k_0, lnx_g_0, lnx_b_0, sgu_ln_g_0, sgu_ln_b_0, sgu_w_0, sgu_b_0, pool_w_0, pool_scale_0, w_out_0, ln_mix_g_0, ln_mix_b_0, ffn_gate_0, ffn_up_0, ffn_down_0, ln_ffn_g_0, ln_ffn_b_0, w_in_1, conv_w_1, shift_mu_1, decay_w0_1, decay_up_1, iclr_a0_1, iclr_up_1, vres_v0_1, vres_up_1, gate_up_1, k_k_1, k_a_1, r_k_1, lnx_g_1, lnx_b_1, sgu_ln_g_1, sgu_ln_b_1, sgu_w_1, sgu_b_1, pool_w_1, pool_scale_1, w_out_1, ln_mix_g_1, ln_mix_b_1, ffn_gate_1, ffn_up_1, ffn_down_1, ln_ffn_g_1, ln_ffn_b_1):
    p0 = dict(zip(_NAMES_0, (w_in_0, conv_w_0, shift_mu_0, decay_w0_0, decay_up_0, iclr_a0_0, iclr_up_0, gate_up_0, k_k_0, k_a_0, r_k_0, lnx_g_0, lnx_b_0, sgu_ln_g_0, sgu_ln_b_0, sgu_w_0, sgu_b_0, pool_w_0, pool_scale_0, w_out_0, ln_mix_g_0, ln_mix_b_0, ffn_gate_0, ffn_up_0, ffn_down_0, ln_ffn_g_0, ln_ffn_b_0)))
    p1 = dict(zip(_NAMES_1, (w_in_1, conv_w_1, shift_mu_1, decay_w0_1, decay_up_1, iclr_a0_1, iclr_up_1, vres_v0_1, vres_up_1, gate_up_1, k_k_1, k_a_1, r_k_1, lnx_g_1, lnx_b_1, sgu_ln_g_1, sgu_ln_b_1, sgu_w_1, sgu_b_1, pool_w_1, pool_scale_1, w_out_1, ln_mix_g_1, ln_mix_b_1, ffn_gate_1, ffn_up_1, ffn_down_1, ln_ffn_g_1, ln_ffn_b_1)))
    B, T, D = x.shape
    assert B == 1
    h = x.reshape(T, D)
    h, v_first, w_in_t = _hybrid_layer(h, None, p0, w_in_0.T.astype(BF16), w_in_1)
    h, _, _ = _hybrid_layer(h, v_first, p1, w_in_t, None)
    return h.reshape(B, T, D)
```

```python
import functools

import jax
import jax.numpy as jnp
from jax import lax
from jax.experimental import pallas as pl
from jax.experimental.pallas import tpu as pltpu

LANES = 128
BF16_SUBLANES = 16
RWKV_HEAD = 64
SGU_CHUNK = 128
SGU_HEAD = 128
CONV_W = 3
POOL_WINDOWS = (2, 4, 8, 16)
POOL_HALO = 16
CONV_HALO = 8
LN_EPS = 1e-5
RWKV_GN_EPS = 64e-5
WKV_CHUNK = 64
RWKV_BLOCK_LANES = 1024
DEPTH = 2
DEEPNORM_ALPHA = (2 * DEPTH) ** 0.25
VMEM_LIMIT_BYTES = 60 * 1024 * 1024

F32 = jnp.float32
BF16 = jnp.bfloat16


def _cparams(sem):
    return pltpu.CompilerParams(dimension_semantics=sem, vmem_limit_bytes=VMEM_LIMIT_BYTES)


def _pick(n, prefs):
    for p in prefs:
        if n % p == 0:
            return p
    return n


def _round_up(n, m):
    return (n + m - 1) // m * m


def _bdot(a, b):
    return jnp.dot(a.astype(BF16), b.astype(BF16), preferred_element_type=F32)


def _layer_norm(x, g, b, eps):
    mu = jnp.mean(x, axis=-1, keepdims=True)
    xc = x - mu
    var = jnp.mean(xc * xc, axis=-1, keepdims=True)
    return xc * lax.rsqrt(var + eps) * g + b


def _sigmoid(x):
    return 1.0 / (1.0 + jnp.exp(-x))


def _gelu_tanh(x):
    c = 0.7978845608028654
    return 0.5 * x * (1.0 + jnp.tanh(c * (x + 0.044715 * (x * x * x))))


def _softplus(x):
    return jnp.maximum(x, 0.0) + jnp.log(1.0 + jnp.exp(-jnp.abs(x)))


def _inproj_kernel(x_ref, wt_ref, o_ref, xb_ref):
    @pl.when(pl.program_id(1) == 0)
    def _():
        xb_ref[...] = x_ref[...].astype(BF16)

    o_ref[...] = lax.dot_general(xb_ref[...], wt_ref[...], (((1,), (1,)), ((), ())), preferred_element_type=F32)


def _inproj(x, wt, head, n_small, width):
    T, D = x.shape
    tm = _pick(T, (1024, 512, 256, 128))
    tail = width - (wt.shape[0] - n_small)
    tn = next(c for c in (512, 256, 128) if head % c == 0 and width % c == 0 and tail == c)
    n_head = head // tn
    n_blocks = width // tn

    assert n_small % BF16_SUBLANES == 0

    def row_start(i, j):
        moved = head + n_small + (j - n_head) * tn
        start = jnp.where(j < n_head, j * tn, jnp.where(j < n_blocks - 1, moved, head))
        return pl.multiple_of(start, BF16_SUBLANES), 0

    return pl.pallas_call(
        _inproj_kernel,
        grid=(T // tm, n_blocks),
        in_specs=[pl.BlockSpec((tm, D), lambda i, j: (i, 0)),
                  pl.BlockSpec((pl.Element(tn), pl.Element(D)), row_start)],
        out_specs=pl.BlockSpec((tm, tn), lambda i, j: (i, j)),
        out_shape=jax.ShapeDtypeStruct((T, width), F32),
        scratch_shapes=[pltpu.VMEM((tm, D), BF16)],
        compiler_params=_cparams(("arbitrary", "arbitrary")),
        name="inproj",
    )(x, wt)


def _local_mixers_kernel(h_ref, bg_ref, cg_ref, su_ref, sv_ref, pz_ref,
                         convw_ref, lng_ref, lnb_ref, sguw_ref, sgubt_ref, poolw_ref, pools_ref,
                         yconv_ref, ysgu_ref, ypool_ref,
                         zbuf_ref, pbuf_ref, *, tb):
    i = pl.program_id(0)

    @pl.when(i == 0)
    def _():
        zbuf_ref[0:CONV_HALO, :] = jnp.zeros((CONV_HALO, zbuf_ref.shape[1]), F32)
        pbuf_ref[0:POOL_HALO, :] = jnp.zeros((POOL_HALO, pbuf_ref.shape[1]), F32)

    zbuf_ref[CONV_HALO:CONV_HALO + tb, :] = cg_ref[...] * h_ref[...]
    conv = zbuf_ref[CONV_HALO - (CONV_W - 1):CONV_HALO - (CONV_W - 1) + tb, :] * convw_ref[0:1, :]
    for j in range(1, CONV_W):
        off = CONV_HALO - (CONV_W - 1 - j)
        conv = conv + zbuf_ref[off:off + tb, :] * convw_ref[j:j + 1, :]
    yconv_ref[...] = (bg_ref[...] * conv).astype(yconv_ref.dtype)
    zbuf_ref[0:CONV_HALO, :] = zbuf_ref[tb:tb + CONV_HALO, :]

    u = _gelu_tanh(su_ref[...])
    v = _layer_norm(_gelu_tanh(sv_ref[...]), lng_ref[...], lnb_ref[...], LN_EPS).astype(BF16)
    n_heads = sguw_ref.shape[0]
    row = lax.broadcasted_iota(jnp.int32, (SGU_CHUNK, SGU_CHUNK), 0)
    col = lax.broadcasted_iota(jnp.int32, (SGU_CHUNK, SGU_CHUNK), 1)
    causal = col <= row
    for hd in range(n_heads):
        w_h = jnp.where(causal, sguw_ref[hd], 0.0).astype(BF16)
        bias = sgubt_ref[:, hd:hd + 1]
        cs = slice(hd * SGU_HEAD, (hd + 1) * SGU_HEAD)
        for c in range(tb // SGU_CHUNK):
            rs = slice(c * SGU_CHUNK, (c + 1) * SGU_CHUNK)
            s = jnp.dot(w_h, v[rs, cs], preferred_element_type=F32) + bias
            ysgu_ref[rs, cs] = (u[rs, cs] * s).astype(ysgu_ref.dtype)

    pbuf_ref[POOL_HALO:POOL_HALO + tb, :] = pz_ref[...]
    pg = poolw_ref.shape[1]
    t_glob = i * tb + lax.broadcasted_iota(jnp.int32, (tb, 1), 0)
    for gi, win in enumerate(POOL_WINDOWS):
        cs = slice(gi * pg, (gi + 1) * pg)
        z = pbuf_ref[POOL_HALO:POOL_HALO + tb, cs]
        acc = z
        for j in range(1, win):
            acc = acc + pbuf_ref[POOL_HALO - j:POOL_HALO - j + tb, cs]
        cnt = jnp.minimum(t_glob + 1, win).astype(F32)
        d = acc / cnt - z
        y = _bdot(d, poolw_ref[gi])
        ypool_ref[:, cs] = (y * pools_ref[:, cs]).astype(ypool_ref.dtype)
    pbuf_ref[0:POOL_HALO, :] = pbuf_ref[tb:tb + POOL_HALO, :]


def _local_mixers(proj, G, p):
    T = proj.shape[0]
    tb = _pick(T, (256, 128))
    n_sgu = G // SGU_HEAD
    col = lambda c: pl.BlockSpec((tb, G), lambda i, c=c: (i, c))
    full = lambda a: pl.BlockSpec(a.shape, lambda i, n=a.ndim: (0,) * n)
    convw = p['conv_w']
    lng = p['sgu_ln_g'].reshape(1, G)
    lnb = p['sgu_ln_b'].reshape(1, G)
    sguw = p['sgu_w']
    sgubt = p['sgu_b'].T
    poolw = p['pool_w'].astype(BF16)
    pools = p['pool_scale'].reshape(1, G)
    outs = pl.pallas_call(
        functools.partial(_local_mixers_kernel, tb=tb),
        grid=(T // tb,),
        in_specs=[col(0), col(1), col(2), col(6), col(7), col(8),
                  full(convw), full(lng), full(lnb), full(sguw), full(sgubt), full(poolw), full(pools)],
        out_specs=[pl.BlockSpec((tb, G), lambda i: (i, 0))] * 3,
        out_shape=[jax.ShapeDtypeStruct((T, G), BF16)] * 3,
        scratch_shapes=[pltpu.VMEM((tb + CONV_HALO, G), F32), pltpu.VMEM((tb + POOL_HALO, G), F32)],
        compiler_params=_cparams(("arbitrary",)),
        name="local_mixers",
    )(proj, proj, proj, proj, proj, proj, convw, lng, lnb, sguw, sgubt, poolw, pools)
    assert n_sgu == sguw.shape[0]
    return outs


def _each(fn, *lists):
    return [fn(*args) for args in zip(*lists)]


def _unit_lower_inverse(Ls, eye, row, col):
    blk16 = (row >> 4) == (col >> 4)
    blk32 = (row >> 5) == (col >> 5)
    C = row.shape[0]
    P = [jnp.where(blk16, L, 0.0) for L in Ls]
    T = [eye + p for p in P]
    P = _each(_pdot, P, P)
    for _ in range(2):
        TP = _each(lambda t, p: _pdot(jnp.concatenate([t, p], axis=0), p), T, P)
        T = _each(lambda t, z: t + z[:C], T, TP)
        P = [z[C:] for z in TP]
    T = _each(lambda t, p: t + _pdot(t, p), T, P)
    for off_diag in (blk32 & jnp.logical_not(blk16), jnp.logical_not(blk32)):
        X = _each(lambda L, t: _pdot(jnp.where(off_diag, L, 0.0), t), Ls, T)
        T = _each(lambda t, x: t + _pdot(t, x), T, X)
    return T


def _block_diag(x):
    x = x.astype(BF16)
    first = lax.broadcasted_iota(jnp.int32, x.shape, 1) < RWKV_HEAD
    zero = jnp.zeros_like(x)
    return jnp.concatenate([jnp.where(first, x, zero), jnp.where(first, zero, x)], axis=0)


def _fold_diag(x):
    first = lax.broadcasted_iota(jnp.int32, (RWKV_HEAD, LANES), 1) < RWKV_HEAD
    return jnp.where(first, x[:RWKV_HEAD], x[RWKV_HEAD:])


def _pdot(a, b):
    return jnp.dot(a.astype(BF16), _block_diag(b), preferred_element_type=F32)


def _pdot_nt(a, b):
    return lax.dot_general(a.astype(BF16), _block_diag(b), (((1,), (1,)), ((), ())), preferred_element_type=F32)


def _pdot_tn(a, b):
    return _fold_diag(_bdot_tn(a, b))


def _bdot_tn(a, b):
    return lax.dot_general(a.astype(BF16), b.astype(BF16), (((0,), (0,)), ((), ())), preferred_element_type=F32)


def _wkv_chunk_operators(at, rt, bt, kt, bh, kh, v):
    C = at[0].shape[0]
    assert C == RWKV_HEAD
    row = lax.broadcasted_iota(jnp.int32, (C, LANES), 0)
    col = lax.broadcasted_iota(jnp.int32, (C, LANES), 1) & (RWKV_HEAD - 1)
    incl = col <= row
    strict = col < row
    eye = jnp.where(row == col, 1.0, 0.0).astype(F32)
    lanes = lambda z: (z[:, :LANES], z[:, LANES:])
    ar = _each(lambda x, y: jnp.concatenate([x, y.astype(BF16)], axis=0), at, rt)
    p = _each(lambda x, b_, k_: lax.dot_general(
        x, jnp.concatenate([_block_diag(b_), _block_diag(k_)], axis=0), (((1,), (1,)), ((), ())),
        preferred_element_type=F32), ar, bt, kt)
    Lab = [jnp.where(strict, z[:C, :LANES], 0.0) for z in p]
    Mak = [jnp.where(strict, z[:C, LANES:], 0.0) for z in p]
    Mrb = [jnp.where(incl, z[C:, :LANES], 0.0).astype(BF16) for z in p]
    Mrk = [jnp.where(incl, z[C:, LANES:], 0.0) for z in p]
    mv = _each(lambda m0, m1, x: _pdot(jnp.concatenate([m0, m1], axis=0), x), Mak, Mrk, v)
    MakV = [z[:C] for z in mv]
    MrkV = [z[C:] for z in mv]
    KV = _each(_pdot_tn, v, kh)
    Tinv = [t.astype(BF16) for t in _unit_lower_inverse(Lab, eye, row, col)]
    pdot2 = lambda x, y0, y1: lanes(jnp.dot(
        x, jnp.concatenate([_block_diag(y0), _block_diag(y1)], axis=1), preferred_element_type=F32))
    wu = _each(pdot2, Tinv, at, MakV)
    W = [z[0].astype(BF16) for z in wu]
    U = [z[1].astype(BF16) for z in wu]
    qy = _each(pdot2, Mrb, W, U)
    Q = _each(lambda x, z: x + z[0], rt, qy)
    y_add = _each(lambda z, m: z[1] + m, qy, MrkV)
    ms = _each(lambda w, u, b_: _bdot_tn(jnp.concatenate([w, u], axis=1), b_), W, U, bh)
    m_state = [_fold_diag(z[:LANES]) for z in ms]
    s_add = _each(lambda z, kv: _fold_diag(z[LANES:]) + kv, ms, KV)
    return Q, y_add, m_state, s_add


def _rwkv_kernel(*refs, tb, has_vres, n_cast):
    n_in = 21 if has_vres else 18
    n_out = 1 if has_vres else 2
    ins, refs = refs[:n_in], refs[n_in:]
    cast_in, refs = refs[:n_cast], refs[n_cast:]
    outs, refs = refs[:n_out], refs[n_out:]
    cast_out, refs = refs[:n_cast], refs[n_cast:]
    prev_ref, prevs_ref, S_ref, ybuf_ref = refs
    if has_vres:
        (r_ref, k_ref, v_ref, sm_ref, vfirst_ref,
         mur_ref, muk_ref, muv_ref, mus_ref,
         w0_ref, wup_ref, a0_ref, aup_ref, gup_ref, v0_ref, vup_ref,
         kk_ref, ka_ref, rk_ref, lng_ref, lnb_ref) = ins
        (y_ref,) = outs
    else:
        (r_ref, k_ref, v_ref, sm_ref,
         mur_ref, muk_ref, muv_ref, mus_ref,
         w0_ref, wup_ref, a0_ref, aup_ref, gup_ref,
         kk_ref, ka_ref, rk_ref, lng_ref, lnb_ref) = ins
        y_ref, vfirst_out_ref = outs
    t = pl.program_id(1)

    for src_ref, dst_ref in zip(cast_in, cast_out):
        dst_ref[...] = src_ref[...].astype(BF16)

    @pl.when(t == 0)
    def _():
        prev_ref[...] = jnp.zeros(prev_ref.shape, F32)
        prevs_ref[...] = jnp.zeros(prevs_ref.shape, F32)
        S_ref[...] = jnp.zeros(S_ref.shape, F32)

    def shift_mix(raw, prev_row, mu):
        first = lax.broadcasted_iota(jnp.int32, raw.shape, 0) == 0
        sh = jnp.where(first, prev_row, pltpu.roll(raw, 1, 0))
        return raw + (sh - raw) * mu

    r_raw, k_raw, v_raw, sm_raw = r_ref[...], k_ref[...], v_ref[...], sm_ref[...]
    r = shift_mix(r_raw, prev_ref[0, 7:8, :], mur_ref[...])
    k = shift_mix(k_raw, prev_ref[1, 7:8, :], muk_ref[...])
    v = shift_mix(v_raw, prev_ref[2, 7:8, :], muv_ref[...])
    sm = shift_mix(sm_raw, prevs_ref[7:8, :], mus_ref[...])
    prev_ref[0] = r_raw[tb - 8:tb, :]
    prev_ref[1] = k_raw[tb - 8:tb, :]
    prev_ref[2] = v_raw[tb - 8:tb, :]
    prevs_ref[...] = sm_raw[tb - 8:tb, :]

    dl, al, gl = wup_ref.shape[0], aup_ref.shape[0], gup_ref.shape[0]
    wd = sm[:, 0:dl]
    ad = sm[:, dl:dl + al]
    gd = sm[:, dl + al:dl + al + gl]
    w = -_softplus(-(w0_ref[...] + _bdot(jnp.tanh(wd), wup_ref[...]))) - 0.5
    lw = -jnp.exp(w)
    a = _sigmoid(a0_ref[...] + _bdot(ad, aup_ref[...]))
    g = _bdot(_sigmoid(gd), gup_ref[...])
    if has_vres:
        ml = vup_ref.shape[0]
        vd = sm[:, dl + al + gl:dl + al + gl + ml]
        v = v + (vfirst_ref[...] - v) * _sigmoid(v0_ref[...] + _bdot(vd, vup_ref[...]))
    else:
        vfirst_out_ref[...] = v

    li = lax.broadcasted_iota(jnp.int32, (LANES, LANES), 0) // RWKV_HEAD
    lj = lax.broadcasted_iota(jnp.int32, (LANES, LANES), 1) // RWKV_HEAD
    head_ones = jnp.where(li == lj, 1.0, 0.0).astype(BF16)
    width = y_ref.shape[1]
    n_pairs = width // LANES

    def split3(z):
        hi = z.astype(BF16)
        r1 = z - hi.astype(F32)
        mid = r1.astype(BF16)
        return hi, mid, (r1 - mid.astype(F32)).astype(BF16)

    def head_sum(z):
        parts = split3(z)[:2]
        return jnp.concatenate(
            [sum(jnp.dot(q[:, j * LANES:(j + 1) * LANES], head_ones, preferred_element_type=F32) for q in parts)
             for j in range(n_pairs)], axis=1)

    kk = k * kk_ref[...]
    kk = kk / jnp.maximum(jnp.sqrt(head_sum(kk * kk)), 1e-12)
    k = k * (1.0 + (a - 1.0) * ka_ref[...])

    C = WKV_CHUNK
    n_chunks = tb // C
    ti = lax.broadcasted_iota(jnp.int32, (tb, tb), 0)
    tj = lax.broadcasted_iota(jnp.int32, (tb, tb), 1)
    tri = jnp.where((tj <= ti) & ((ti // C) == (tj // C)), 1.0, 0.0).astype(BF16)
    lc = sum(jnp.dot(tri, q, preferred_element_type=F32) for q in split3(lw))
    lc_end = jnp.concatenate(
        [jnp.broadcast_to(lc[(c + 1) * C - 1:(c + 1) * C, :], (C, width)) for c in range(n_chunks)], axis=0)
    e_neg = jnp.exp(-lc)
    e_end = jnp.exp(lc_end - lc)
    at = (-kk * jnp.exp(lc - lw)).astype(BF16)
    rt = r * jnp.exp(lc)
    bt = (kk * a * e_neg).astype(BF16)
    kt = (k * e_neg).astype(BF16)
    bh = (kk * a * e_end).astype(BF16)
    kh = (k * e_end).astype(BF16)
    vb = v.astype(BF16)
    decay_end = jnp.exp(lc_end)

    tiles = [(c, j) for c in range(n_chunks) for j in range(n_pairs)]
    cut = lambda z: [z[c * C:(c + 1) * C, j * LANES:(j + 1) * LANES] for c, j in tiles]
    Q, y_add, m_state, s_add = _wkv_chunk_operators(*[cut(z) for z in (at, rt, bt, kt, bh, kh, vb)])
    S = [S_ref[j] for j in range(n_pairs)]
    for c in range(n_chunks):
        for j in range(n_pairs):
            i = c * n_pairs + j
            ls = slice(j * LANES, (j + 1) * LANES)
            ybuf_ref[c * C:(c + 1) * C, ls] = _pdot_nt(Q[i], S[j]) + y_add[i]
            S[j] = S[j] * decay_end[c * C:c * C + 1, ls] + _pdot(S[j], m_state[i]) + s_add[i]
    for j in range(n_pairs):
        S_ref[j] = S[j]

    y = ybuf_ref[...]
    inv_n = 1.0 / RWKV_HEAD
    mu = head_sum(y) * inv_n
    yc = y - mu
    var = head_sum(yc * yc) * inv_n
    y = yc * lax.rsqrt(var + RWKV_GN_EPS) * lng_ref[...] + lnb_ref[...]
    y = y + head_sum(r * k * rk_ref[...]) * v
    y_ref[...] = (y * g).astype(y_ref.dtype)


def _cast_plan(shape, n_steps):
    rows, cols = shape
    for ncb in (1, 2, 4, 8, 16):
        rb = n_steps // ncb
        if (n_steps % ncb == 0 and rows % rb == 0 and cols % ncb == 0
                and (rows // rb) % BF16_SUBLANES == 0 and (cols // ncb) % LANES == 0):
            return rb, ncb
    return None


def _rwkv_mixer(proj, G, small_w, small_off, p, v_first, to_cast):
    T = proj.shape[0]
    has_vres = v_first is not None
    tb = _pick(T, (256, 128, 64))
    bw = _pick(G, (RWKV_BLOCK_LANES, 2 * LANES, LANES))
    cb = G // bw
    dl, al, gl = p['decay_up'].shape[0], p['iclr_up'].shape[0], p['gate_up'].shape[0]
    ml = p['vres_up'].shape[0] if has_vres else 0
    mu = p['shift_mu']
    mu_r, mu_k, mu_v = (mu[q * G:(q + 1) * G].reshape(1, G) for q in range(3))
    mu_s = jnp.pad(mu[3 * G:], (0, small_w - (dl + al + gl + ml))).reshape(1, small_w)

    def colblk(first):
        return pl.BlockSpec((tb, bw), lambda q, t, f=first: (t, f + q))

    def vec(arr):
        return arr.reshape(1, G), pl.BlockSpec((1, bw), lambda q, t: (0, q))

    def up(arr):
        return arr.astype(BF16), pl.BlockSpec((arr.shape[0], bw), lambda q, t: (0, q))

    small_spec = pl.BlockSpec((tb, small_w), lambda q, t: (t, small_off // small_w))
    pair_spec = pl.BlockSpec((tb, bw), lambda q, t: (t, q))
    mu_spec = pl.BlockSpec((1, bw), lambda q, t: (0, q))
    args = [proj, proj, proj, proj]
    specs = [colblk(3 * cb), colblk(4 * cb), colblk(5 * cb), small_spec]
    if has_vres:
        args.append(v_first)
        specs.append(pair_spec)
    args += [mu_r, mu_k, mu_v, mu_s]
    specs += [mu_spec, mu_spec, mu_spec, pl.BlockSpec((1, small_w), lambda q, t: (0, 0))]
    names = ['decay_w0', 'decay_up', 'iclr_a0', 'iclr_up', 'gate_up']
    if has_vres:
        names += ['vres_v0', 'vres_up']
    names += ['k_k', 'k_a', 'r_k', 'lnx_g', 'lnx_b']
    for nm in names:
        arr, spec = up(p[nm]) if nm.endswith('_up') else vec(p[nm])
        args.append(arr)
        specs.append(spec)
    out_shape = [jax.ShapeDtypeStruct((T, G), BF16)]
    out_specs = [pair_spec]
    if not has_vres:
        out_shape.append(jax.ShapeDtypeStruct((T, G), F32))
        out_specs.append(pair_spec)
    n_t = T // tb
    n_steps = (G // bw) * n_t
    cast_names, cast_out = [], {}
    for nm, arr in to_cast.items():
        plan = _cast_plan(arr.shape, n_steps)
        if plan is None:
            cast_out[nm] = arr.astype(BF16)
            continue
        rb, ncb = plan
        spec = pl.BlockSpec((arr.shape[0] // rb, arr.shape[1] // ncb),
                            lambda q, t, ncb=ncb: ((q * n_t + t) // ncb, (q * n_t + t) % ncb))
        cast_names.append(nm)
        args.append(arr)
        specs.append(spec)
        out_shape.append(jax.ShapeDtypeStruct(arr.shape, BF16))
        out_specs.append(spec)
    n_main_out = 1 if has_vres else 2
    outs = pl.pallas_call(
        functools.partial(_rwkv_kernel, tb=tb, has_vres=has_vres, n_cast=len(cast_names)),
        grid=(G // bw, T // tb),
        in_specs=specs,
        out_specs=out_specs,
        out_shape=out_shape,
        scratch_shapes=[pltpu.VMEM((3, 8, bw), F32), pltpu.VMEM((8, small_w), F32),
                        pltpu.VMEM((bw // LANES, RWKV_HEAD, LANES), F32),
                        pltpu.VMEM((tb, bw), F32)],
        compiler_params=_cparams(("arbitrary", "arbitrary")),
        name="rwkv7_mixer",
    )(*args)
    cast_out.update(zip(cast_names, outs[n_main_out:]))
    return outs[0], (v_first if has_vres else outs[1]), cast_out


def _outproj_kernel(y0_ref, y1_ref, y2_ref, y3_ref, w_ref, x_ref, g_ref, b_ref, o_ref, ycat_ref, *, tn):
    G = y0_ref.shape[1]
    for j, y_ref in enumerate((y0_ref, y1_ref, y2_ref, y3_ref)):
        ycat_ref[:, j * G:(j + 1) * G] = y_ref[...]
    ycat = ycat_ref[...]
    for n in range(0, o_ref.shape[1], tn):
        o_ref[:, n:n + tn] = (DEEPNORM_ALPHA * x_ref[:, n:n + tn]
                              + jnp.dot(ycat, w_ref[:, n:n + tn], preferred_element_type=F32))
    o_ref[...] = _layer_norm(o_ref[...], g_ref[...], b_ref[...], LN_EPS)


def _outproj_ln(ys, w, x, g, b):
    T, D = x.shape
    G = ys[0].shape[1]
    tm = _pick(T, (256, 128))
    yspec = pl.BlockSpec((tm, G), lambda i: (i, 0))
    row = pl.BlockSpec((tm, D), lambda i: (i, 0))
    vec = pl.BlockSpec((1, D), lambda i: (0, 0))
    wspec = pl.BlockSpec(w.shape, lambda i: (0, 0), pipeline_mode=pl.Buffered(1))
    return pl.pallas_call(
        functools.partial(_outproj_kernel, tn=_pick(D, (512, 256, 128))),
        grid=(T // tm,),
        in_specs=[yspec] * 4 + [wspec, row, vec, vec],
        out_specs=row,
        out_shape=jax.ShapeDtypeStruct((T, D), F32),
        scratch_shapes=[pltpu.VMEM((tm, len(ys) * G), BF16)],
        compiler_params=_cparams(("arbitrary",)),
        name="outproj_ln",
    )(*ys, w, x, g.reshape(1, D), b.reshape(1, D))


def _ffn_kernel(x_hbm, wg_ref, wu_ref, wd_ref, g_ref, b_ref, o_hbm, acc_ref, xb_ref, in_sem, out_sem, *, tn):
    i, ni = pl.program_id(0), pl.num_programs(0)
    f, nf = pl.program_id(1), pl.num_programs(1)
    tm = xb_ref.shape[0]
    tr = _pick(tm, (256, 128))
    slot = i % 2

    def rows(c):
        return pl.ds(pl.multiple_of(c * tr, tr), tr)

    def x_copy(blk, s):
        return pltpu.make_async_copy(x_hbm.at[pl.ds(blk * tm, tm), :], acc_ref.at[s], in_sem.at[0])

    def out_copy(blk, s):
        return pltpu.make_async_copy(acc_ref.at[s], o_hbm.at[pl.ds(blk * tm, tm), :], out_sem.at[s])

    @pl.when(f == 0)
    def _():
        @pl.when(i == 0)
        def _():
            x_copy(0, 0).start()

        x_copy(i, slot).wait()

        def body(c, carry):
            xr = acc_ref[slot, rows(c), :]
            xb_ref[rows(c), :] = xr.astype(BF16)
            acc_ref[slot, rows(c), :] = DEEPNORM_ALPHA * xr
            return carry
        lax.fori_loop(0, tm // tr, body, 0)

    xb = xb_ref[...]
    gate = jnp.dot(xb, wg_ref[...], preferred_element_type=F32)
    upv = jnp.dot(xb, wu_ref[...], preferred_element_type=F32)
    hid = (gate * _sigmoid(gate) * upv).astype(BF16)
    for n in range(0, acc_ref.shape[2], tn):
        acc_ref[slot, :, n:n + tn] += jnp.dot(hid, wd_ref[:, n:n + tn], preferred_element_type=F32)

    @pl.when(f == jnp.maximum(nf - 2, 0))
    def _():
        @pl.when(i >= 1)
        def _():
            out_copy(i - 1, 1 - slot).wait()

        @pl.when(i + 1 < ni)
        def _():
            x_copy(i + 1, 1 - slot).start()

    @pl.when(f == nf - 1)
    def _():
        def body(c, carry):
            acc_ref[slot, rows(c), :] = _layer_norm(acc_ref[slot, rows(c), :], g_ref[...], b_ref[...], LN_EPS)
            return carry
        lax.fori_loop(0, tm // tr, body, 0)
        out_copy(i, slot).start()

        @pl.when(i == ni - 1)
        def _():
            out_copy(i, slot).wait()


def _ffn_ln(x, wg, wu, wd, g, b):
    T, D = x.shape
    F = wg.shape[1]
    tm = _pick(T, (1024, 512, 256, 128))
    tf = _pick(F, (256, 128))
    vec = pl.BlockSpec((1, D), lambda i, f: (0, 0))
    return pl.pallas_call(
        functools.partial(_ffn_kernel, tn=_pick(D, (512, 256, 128))),
        grid=(T // tm, F // tf),
        in_specs=[pl.BlockSpec(memory_space=pl.ANY),
                  pl.BlockSpec((D, tf), lambda i, f: (0, f)), pl.BlockSpec((D, tf), lambda i, f: (0, f)),
                  pl.BlockSpec((tf, D), lambda i, f: (f, 0)), vec, vec],
        out_specs=pl.BlockSpec(memory_space=pl.ANY),
        out_shape=jax.ShapeDtypeStruct((T, D), F32),
        scratch_shapes=[pltpu.VMEM((2, tm, D), F32), pltpu.VMEM((tm, D), BF16),
                        pltpu.SemaphoreType.DMA((1,)), pltpu.SemaphoreType.DMA((2,))],
        compiler_params=_cparams(("arbitrary", "arbitrary")),
        name="ffn_ln",
    )(x, wg, wu, wd, g.reshape(1, D), b.reshape(1, D))


def _hybrid_layer(x, v_first, p, w_in_t, next_w_in):
    T, D = x.shape
    G = D // 4
    n_small = w_in_t.shape[0] - 9 * G
    small_w = _round_up(n_small, 2 * LANES)
    proj = _inproj(x, w_in_t, 6 * G, n_small, 9 * G + small_w)
    y_conv, y_sgu, y_pool = _local_mixers(proj, G, p)
    later = {nm: p[nm] for nm in ('w_out', 'ffn_gate', 'ffn_up', 'ffn_down')}
    if next_w_in is not None:
        later['w_in_next_t'] = next_w_in.T
    y_rwkv, v_first, wb = _rwkv_mixer(proj, G, small_w, 9 * G, p, v_first, later)
    x = _outproj_ln((y_conv, y_rwkv, y_sgu, y_pool), wb['w_out'], x, p['ln_mix_g'], p['ln_mix_b'])
    x = _ffn_ln(x, wb['ffn_gate'], wb['ffn_up'], wb['ffn_down'], p['ln_ffn_g'], p['ln_ffn_b'])
    return x, v_first, wb.get('w_in_next_t')


_NAMES_0 = ('w_in', 'conv_w', 'shift_mu', 'decay_w0', 'decay_up', 'iclr_a0', 'iclr_up', 'gate_up',
            'k_k', 'k_a', 'r_k', 'lnx_g', 'lnx_b', 'sgu_ln_g', 'sgu_ln_b', 'sgu_w', 'sgu_b',
            'pool_w', 'pool_scale', 'w_out', 'ln_mix_g', 'ln_mix_b', 'ffn_gate', 'ffn_up', 'ffn_down',
            'ln_ffn_g', 'ln_ffn_b')
_NAMES_1 = _NAMES_0[:7] + ('vres_v0', 'vres_up') + _NAMES_0[7:]


def kernel(x, w_in_0, conv_w_0, shift_mu_0, decay_w0_0, decay_up_0, iclr_a0_0, iclr_up_0, gate_up_0, k_k_0, k_a_0, r_k_0, lnx_g_0, lnx_b_0, sgu_ln_g_0, sgu_ln_b_0, sgu_w_0, sgu_b_0, pool_w_0, pool_scale_0, w_out_0, ln_mix_g_0, ln_mix_b_0, ffn_gate_0, ffn_up_0, ffn_down_0, ln_ffn_g_0, ln_ffn_b_0, w_in_1, conv_w_1, shift_mu_1, decay_w0_1, decay_up_1, iclr_a0_1, iclr_up_1, vres_v0_1, vres_up_1, gate_up_1, k_k_1, k_a_1, r_k_1, lnx_g_1, lnx_b_1, sgu_ln_g_1, sgu_ln_b_1, sgu_w_1, sgu_b_1, pool_w_1, pool_scale_1, w_out_1, ln_mix_g_1, ln_mix_b_1, ffn_gate_1, ffn_up_1, ffn_down_1, ln_ffn_g_1, ln_ffn_b_1):
    p0 = dict(zip(_NAMES_0, (w_in_0, conv_w_0, shift_mu_0, decay_w0_0, decay_up_0, iclr_a0_0, iclr_up_0, gate_up_0, k_k_0, k_a_0, r_k_0, lnx_g_0, lnx_b_0, sgu_ln_g_0, sgu_ln_b_0, sgu_w_0, sgu_b_0, pool_w_0, pool_scale_0, w_out_0, ln_mix_g_0, ln_mix_b_0, ffn_gate_0, ffn_up_0, ffn_down_0, ln_ffn_g_0, ln_ffn_b_0)))
    p1 = dict(zip(_NAMES_1, (w_in_1, conv_w_1, shift_mu_1, decay_w0_1, decay_up_1, iclr_a0_1, iclr_up_1, vres_v0_1, vres_up_1, gate_up_1, k_k_1, k_a_1, r_k_1, lnx_g_1, lnx_b_1, sgu_ln_g_1, sgu_ln_b_1, sgu_w_1, sgu_b_1, pool_w_1, pool_scale_1, w_out_1, ln_mix_g_1, ln_mix_b_1, ffn_gate_1, ffn_up_1, ffn_down_1, ln_ffn_g_1, ln_ffn_b_1)))
    B, T, D = x.shape
    assert B == 1
    h = x.reshape(T, D)
    h, v_first, w_in_t = _hybrid_layer(h, None, p0, w_in_0.T.astype(BF16), w_in_1)
    h, _, _ = _hybrid_layer(h, v_first, p1, w_in_t, None)
    return h.reshape(B, T, D)
```

```python
import functools

import jax
import jax.numpy as jnp
from jax import lax
from jax.experimental import pallas as pl
from jax.experimental.pallas import tpu as pltpu

LANES = 128
BF16_SUBLANES = 16
RWKV_HEAD = 64
SGU_CHUNK = 128
SGU_HEAD = 128
CONV_W = 3
POOL_WINDOWS = (2, 4, 8, 16)
POOL_HALO = 16
CONV_HALO = 8
LN_EPS = 1e-5
RWKV_GN_EPS = 64e-5
WKV_CHUNK = 64
RWKV_BLOCK_LANES = 1024
DEPTH = 2
DEEPNORM_ALPHA = (2 * DEPTH) ** 0.25
VMEM_LIMIT_BYTES = 60 * 1024 * 1024

F32 = jnp.float32
BF16 = jnp.bfloat16


def _cparams(sem):
    return pltpu.CompilerParams(dimension_semantics=sem, vmem_limit_bytes=VMEM_LIMIT_BYTES)


def _pick(n, prefs):
    for p in prefs:
        if n % p == 0:
            return p
    return n


def _round_up(n, m):
    return (n + m - 1) // m * m


def _bdot(a, b):
    return jnp.dot(a.astype(BF16), b.astype(BF16), preferred_element_type=F32)


def _layer_norm(x, g, b, eps):
    mu = jnp.mean(x, axis=-1, keepdims=True)
    xc = x - mu
    var = jnp.mean(xc * xc, axis=-1, keepdims=True)
    return xc * lax.rsqrt(var + eps) * g + b


def _sigmoid(x):
    return 1.0 / (1.0 + jnp.exp(-x))


def _gelu_tanh(x):
    c = 0.7978845608028654
    return 0.5 * x * (1.0 + jnp.tanh(c * (x + 0.044715 * (x * x * x))))


def _softplus(x):
    return jnp.maximum(x, 0.0) + jnp.log(1.0 + jnp.exp(-jnp.abs(x)))


def _inproj_kernel(x_hbm, wt_ref, o_ref, xs_ref, xb_ref, sem):
    i, ni = pl.program_id(0), pl.num_programs(0)
    tm = xb_ref.shape[0]

    def x_copy(blk):
        return pltpu.make_async_copy(x_hbm.at[pl.ds(blk * tm, tm), :], xs_ref, sem.at[0])

    @pl.when(pl.program_id(1) == 0)
    def _():
        @pl.when(i == 0)
        def _():
            x_copy(0).start()

        x_copy(i).wait()
        xb_ref[...] = xs_ref[...].astype(BF16)

        @pl.when(i + 1 < ni)
        def _():
            x_copy(i + 1).start()

    o_ref[...] = lax.dot_general(xb_ref[...], wt_ref[...], (((1,), (1,)), ((), ())), preferred_element_type=F32)


def _inproj(x, wt, head, n_small, width):
    T, D = x.shape
    tm = _pick(T, (1024, 512, 256, 128))
    tail = width - (wt.shape[0] - n_small)
    tn = next(c for c in (512, 256, 128) if head % c == 0 and width % c == 0 and tail == c)
    n_head = head // tn
    n_blocks = width // tn

    assert n_small % BF16_SUBLANES == 0

    def row_start(i, j):
        moved = head + n_small + (j - n_head) * tn
        start = jnp.where(j < n_head, j * tn, jnp.where(j < n_blocks - 1, moved, head))
        return pl.multiple_of(start, BF16_SUBLANES), 0

    return pl.pallas_call(
        _inproj_kernel,
        grid=(T // tm, n_blocks),
        in_specs=[pl.BlockSpec(memory_space=pl.ANY),
                  pl.BlockSpec((pl.Element(tn), pl.Element(D)), row_start)],
        out_specs=pl.BlockSpec((tm, tn), lambda i, j: (i, j)),
        out_shape=jax.ShapeDtypeStruct((T, width), F32),
        scratch_shapes=[pltpu.VMEM((tm, D), F32), pltpu.VMEM((tm, D), BF16), pltpu.SemaphoreType.DMA((1,))],
        compiler_params=_cparams(("arbitrary", "arbitrary")),
        name="inproj",
    )(x, wt)


def _local_mixers_kernel(h_ref, bg_ref, cg_ref, su_ref, sv_ref, pz_ref,
                         convw_ref, lng_ref, lnb_ref, sguw_ref, sgubt_ref, poolw_ref, pools_ref,
                         yconv_ref, ysgu_ref, ypool_ref,
                         zbuf_ref, pbuf_ref, *, tb):
    i = pl.program_id(0)

    @pl.when(i == 0)
    def _():
        zbuf_ref[0:CONV_HALO, :] = jnp.zeros((CONV_HALO, zbuf_ref.shape[1]), F32)
        pbuf_ref[0:POOL_HALO, :] = jnp.zeros((POOL_HALO, pbuf_ref.shape[1]), F32)

    zbuf_ref[CONV_HALO:CONV_HALO + tb, :] = cg_ref[...] * h_ref[...]
    conv = zbuf_ref[CONV_HALO - (CONV_W - 1):CONV_HALO - (CONV_W - 1) + tb, :] * convw_ref[0:1, :]
    for j in range(1, CONV_W):
        off = CONV_HALO - (CONV_W - 1 - j)
        conv = conv + zbuf_ref[off:off + tb, :] * convw_ref[j:j + 1, :]
    yconv_ref[...] = (bg_ref[...] * conv).astype(yconv_ref.dtype)
    zbuf_ref[0:CONV_HALO, :] = zbuf_ref[tb:tb + CONV_HALO, :]

    u = _gelu_tanh(su_ref[...])
    v = _layer_norm(_gelu_tanh(sv_ref[...]), lng_ref[...], lnb_ref[...], LN_EPS).astype(BF16)
    n_heads = sguw_ref.shape[0]
    row = lax.broadcasted_iota(jnp.int32, (SGU_CHUNK, SGU_CHUNK), 0)
    col = lax.broadcasted_iota(jnp.int32, (SGU_CHUNK, SGU_CHUNK), 1)
    causal = col <= row
    for hd in range(n_heads):
        w_h = jnp.where(causal, sguw_ref[hd], 0.0).astype(BF16)
        bias = sgubt_ref[:, hd:hd + 1]
        cs = slice(hd * SGU_HEAD, (hd + 1) * SGU_HEAD)
        for c in range(tb // SGU_CHUNK):
            rs = slice(c * SGU_CHUNK, (c + 1) * SGU_CHUNK)
            s = jnp.dot(w_h, v[rs, cs], preferred_element_type=F32) + bias
            ysgu_ref[rs, cs] = (u[rs, cs] * s).astype(ysgu_ref.dtype)

    pbuf_ref[POOL_HALO:POOL_HALO + tb, :] = pz_ref[...]
    pg = poolw_ref.shape[1]
    t_glob = i * tb + lax.broadcasted_iota(jnp.int32, (tb, 1), 0)
    for gi, win in enumerate(POOL_WINDOWS):
        cs = slice(gi * pg, (gi + 1) * pg)
        z = pbuf_ref[POOL_HALO:POOL_HALO + tb, cs]
        acc = z
        for j in range(1, win):
            acc = acc + pbuf_ref[POOL_HALO - j:POOL_HALO - j + tb, cs]
        cnt = jnp.minimum(t_glob + 1, win).astype(F32)
        d = acc / cnt - z
        y = _bdot(d, poolw_ref[gi])
        ypool_ref[:, cs] = (y * pools_ref[:, cs]).astype(ypool_ref.dtype)
    pbuf_ref[0:POOL_HALO, :] = pbuf_ref[tb:tb + POOL_HALO, :]


def _local_mixers(proj, G, p):
    T = proj.shape[0]
    tb = _pick(T, (256, 128))
    n_sgu = G // SGU_HEAD
    col = lambda c: pl.BlockSpec((tb, G), lambda i, c=c: (i, c))
    full = lambda a: pl.BlockSpec(a.shape, lambda i, n=a.ndim: (0,) * n)
    convw = p['conv_w']
    lng = p['sgu_ln_g'].reshape(1, G)
    lnb = p['sgu_ln_b'].reshape(1, G)
    sguw = p['sgu_w']
    sgubt = p['sgu_b'].T
    poolw = p['pool_w'].astype(BF16)
    pools = p['pool_scale'].reshape(1, G)
    outs = pl.pallas_call(
        functools.partial(_local_mixers_kernel, tb=tb),
        grid=(T // tb,),
        in_specs=[col(0), col(1), col(2), col(6), col(7), col(8),
                  full(convw), full(lng), full(lnb), full(sguw), full(sgubt), full(poolw), full(pools)],
        out_specs=[pl.BlockSpec((tb, G), lambda i: (i, 0))] * 3,
        out_shape=[jax.ShapeDtypeStruct((T, G), BF16)] * 3,
        scratch_shapes=[pltpu.VMEM((tb + CONV_HALO, G), F32), pltpu.VMEM((tb + POOL_HALO, G), F32)],
        compiler_params=_cparams(("arbitrary",)),
        name="local_mixers",
    )(proj, proj, proj, proj, proj, proj, convw, lng, lnb, sguw, sgubt, poolw, pools)
    assert n_sgu == sguw.shape[0]
    return outs


def _each(fn, *lists):
    return [fn(*args) for args in zip(*lists)]


def _unit_lower_inverse(Ls, eye, row, col):
    blk16 = (row >> 4) == (col >> 4)
    blk32 = (row >> 5) == (col >> 5)
    C = row.shape[0]
    P = [jnp.where(blk16, L, 0.0) for L in Ls]
    T = [eye + p for p in P]
    P = _each(_pdot, P, P)
    for _ in range(2):
        TP = _each(lambda t, p: _pdot(jnp.concatenate([t, p], axis=0), p), T, P)
        T = _each(lambda t, z: t + z[:C], T, TP)
        P = [z[C:] for z in TP]
    T = _each(lambda t, p: t + _pdot(t, p), T, P)
    for off_diag in (blk32 & jnp.logical_not(blk16), jnp.logical_not(blk32)):
        X = _each(lambda L, t: _pdot(jnp.where(off_diag, L, 0.0), t), Ls, T)
        T = _each(lambda t, x: t + _pdot(t, x), T, X)
    return T


def _block_diag(x):
    x = x.astype(BF16)
    first = lax.broadcasted_iota(jnp.int32, x.shape, 1) < RWKV_HEAD
    zero = jnp.zeros_like(x)
    return jnp.concatenate([jnp.where(first, x, zero), jnp.where(first, zero, x)], axis=0)


def _fold_diag(x):
    first = lax.broadcasted_iota(jnp.int32, (RWKV_HEAD, LANES), 1) < RWKV_HEAD
    return jnp.where(first, x[:RWKV_HEAD], x[RWKV_HEAD:])


def _pdot(a, b):
    return jnp.dot(a.astype(BF16), _block_diag(b), preferred_element_type=F32)


def _pdot_nt(a, b):
    return lax.dot_general(a.astype(BF16), _block_diag(b), (((1,), (1,)), ((), ())), preferred_element_type=F32)


def _pdot_tn(a, b):
    return _fold_diag(_bdot_tn(a, b))


def _bdot_tn(a, b):
    return lax.dot_general(a.astype(BF16), b.astype(BF16), (((0,), (0,)), ((), ())), preferred_element_type=F32)


def _wkv_chunk_operators(at, rt, bt, kt, bh, kh, v):
    C = at[0].shape[0]
    assert C == RWKV_HEAD
    row = lax.broadcasted_iota(jnp.int32, (C, LANES), 0)
    col = lax.broadcasted_iota(jnp.int32, (C, LANES), 1) & (RWKV_HEAD - 1)
    incl = col <= row
    strict = col < row
    eye = jnp.where(row == col, 1.0, 0.0).astype(F32)
    lanes = lambda z: (z[:, :LANES], z[:, LANES:])
    ar = _each(lambda x, y: jnp.concatenate([x, y.astype(BF16)], axis=0), at, rt)
    p = _each(lambda x, b_, k_: lax.dot_general(
        x, jnp.concatenate([_block_diag(b_), _block_diag(k_)], axis=0), (((1,), (1,)), ((), ())),
        preferred_element_type=F32), ar, bt, kt)
    Lab = [jnp.where(strict, z[:C, :LANES], 0.0) for z in p]
    Mak = [jnp.where(strict, z[:C, LANES:], 0.0) for z in p]
    Mrb = [jnp.where(incl, z[C:, :LANES], 0.0).astype(BF16) for z in p]
    Mrk = [jnp.where(incl, z[C:, LANES:], 0.0) for z in p]
    mv = _each(lambda m0, m1, x: _pdot(jnp.concatenate([m0, m1], axis=0), x), Mak, Mrk, v)
    MakV = [z[:C] for z in mv]
    MrkV = [z[C:] for z in mv]
    KV = _each(_pdot_tn, v, kh)
    Tinv = [t.astype(BF16) for t in _unit_lower_inverse(Lab, eye, row, col)]
    pdot2 = lambda x, y0, y1: lanes(jnp.dot(
        x, jnp.concatenate([_block_diag(y0), _block_diag(y1)], axis=1), preferred_element_type=F32))
    wu = _each(pdot2, Tinv, at, MakV)
    W = [z[0].astype(BF16) for z in wu]
    U = [z[1].astype(BF16) for z in wu]
    qy = _each(pdot2, Mrb, W, U)
    Q = _each(lambda x, z: x + z[0], rt, qy)
    y_add = _each(lambda z, m: z[1] + m, qy, MrkV)
    ms = _each(lambda w, u, b_: _bdot_tn(jnp.concatenate([w, u], axis=1), b_), W, U, bh)
    m_state = [_fold_diag(z[:LANES]) for z in ms]
    s_add = _each(lambda z, kv: _fold_diag(z[LANES:]) + kv, ms, KV)
    return Q, y_add, m_state, s_add


def _rwkv_kernel(*refs, tb, has_vres, n_cast):
    n_in = 21 if has_vres else 18
    n_out = 1 if has_vres else 2
    ins, refs = refs[:n_in], refs[n_in:]
    cast_in, refs = refs[:n_cast], refs[n_cast:]
    outs, refs = refs[:n_out], refs[n_out:]
    cast_out, refs = refs[:n_cast], refs[n_cast:]
    prev_ref, prevs_ref, S_ref, ybuf_ref = refs
    if has_vres:
        (r_ref, k_ref, v_ref, sm_ref, vfirst_ref,
         mur_ref, muk_ref, muv_ref, mus_ref,
         w0_ref, wup_ref, a0_ref, aup_ref, gup_ref, v0_ref, vup_ref,
         kk_ref, ka_ref, rk_ref, lng_ref, lnb_ref) = ins
        (y_ref,) = outs
    else:
        (r_ref, k_ref, v_ref, sm_ref,
         mur_ref, muk_ref, muv_ref, mus_ref,
         w0_ref, wup_ref, a0_ref, aup_ref, gup_ref,
         kk_ref, ka_ref, rk_ref, lng_ref, lnb_ref) = ins
        y_ref, vfirst_out_ref = outs
    t = pl.program_id(1)

    for src_ref, dst_ref in zip(cast_in, cast_out):
        dst_ref[...] = src_ref[...].astype(BF16)

    @pl.when(t == 0)
    def _():
        prev_ref[...] = jnp.zeros(prev_ref.shape, F32)
        prevs_ref[...] = jnp.zeros(prevs_ref.shape, F32)
        S_ref[...] = jnp.zeros(S_ref.shape, F32)

    def shift_mix(raw, prev_row, mu):
        first = lax.broadcasted_iota(jnp.int32, raw.shape, 0) == 0
        sh = jnp.where(first, prev_row, pltpu.roll(raw, 1, 0))
        return raw + (sh - raw) * mu

    r_raw, k_raw, v_raw, sm_raw = r_ref[...], k_ref[...], v_ref[...], sm_ref[...]
    r = shift_mix(r_raw, prev_ref[0, 7:8, :], mur_ref[...])
    k = shift_mix(k_raw, prev_ref[1, 7:8, :], muk_ref[...])
    v = shift_mix(v_raw, prev_ref[2, 7:8, :], muv_ref[...])
    sm = shift_mix(sm_raw, prevs_ref[7:8, :], mus_ref[...])
    prev_ref[0] = r_raw[tb - 8:tb, :]
    prev_ref[1] = k_raw[tb - 8:tb, :]
    prev_ref[2] = v_raw[tb - 8:tb, :]
    prevs_ref[...] = sm_raw[tb - 8:tb, :]

    dl, al, gl = wup_ref.shape[0], aup_ref.shape[0], gup_ref.shape[0]
    wd = sm[:, 0:dl]
    ad = sm[:, dl:dl + al]
    gd = sm[:, dl + al:dl + al + gl]
    w = -_softplus(-(w0_ref[...] + _bdot(jnp.tanh(wd), wup_ref[...]))) - 0.5
    lw = -jnp.exp(w)
    a = _sigmoid(a0_ref[...] + _bdot(ad, aup_ref[...]))
    g = _bdot(_sigmoid(gd), gup_ref[...])
    if has_vres:
        ml = vup_ref.shape[0]
        vd = sm[:, dl + al + gl:dl + al + gl + ml]
        v = v + (vfirst_ref[...] - v) * _sigmoid(v0_ref[...] + _bdot(vd, vup_ref[...]))
    else:
        vfirst_out_ref[...] = v

    li = lax.broadcasted_iota(jnp.int32, (LANES, LANES), 0) // RWKV_HEAD
    lj = lax.broadcasted_iota(jnp.int32, (LANES, LANES), 1) // RWKV_HEAD
    head_ones = jnp.where(li == lj, 1.0, 0.0).astype(BF16)
    width = y_ref.shape[1]
    n_pairs = width // LANES

    def split3(z):
        hi = z.astype(BF16)
        r1 = z - hi.astype(F32)
        mid = r1.astype(BF16)
        return hi, mid, (r1 - mid.astype(F32)).astype(BF16)

    def head_sum(z):
        parts = split3(z)[:2]
        return jnp.concatenate(
            [sum(jnp.dot(q[:, j * LANES:(j + 1) * LANES], head_ones, preferred_element_type=F32) for q in parts)
             for j in range(n_pairs)], axis=1)

    kk = k * kk_ref[...]
    kk = kk / jnp.maximum(jnp.sqrt(head_sum(kk * kk)), 1e-12)
    k = k * (1.0 + (a - 1.0) * ka_ref[...])

    C = WKV_CHUNK
    n_chunks = tb // C
    ti = lax.broadcasted_iota(jnp.int32, (tb, tb), 0)
    tj = lax.broadcasted_iota(jnp.int32, (tb, tb), 1)
    tri = jnp.where((tj <= ti) & ((ti // C) == (tj // C)), 1.0, 0.0).astype(BF16)
    lc = sum(jnp.dot(tri, q, preferred_element_type=F32) for q in split3(lw))
    lc_end = jnp.concatenate(
        [jnp.broadcast_to(lc[(c + 1) * C - 1:(c + 1) * C, :], (C, width)) for c in range(n_chunks)], axis=0)
    e_neg = jnp.exp(-lc)
    e_end = jnp.exp(lc_end - lc)
    at = (-kk * jnp.exp(lc - lw)).astype(BF16)
    rt = r * jnp.exp(lc)
    bt = (kk * a * e_neg).astype(BF16)
    kt = (k * e_neg).astype(BF16)
    bh = (kk * a * e_end).astype(BF16)
    kh = (k * e_end).astype(BF16)
    vb = v.astype(BF16)
    decay_end = jnp.exp(lc_end)

    tiles = [(c, j) for c in range(n_chunks) for j in range(n_pairs)]
    cut = lambda z: [z[c * C:(c + 1) * C, j * LANES:(j + 1) * LANES] for c, j in tiles]
    Q, y_add, m_state, s_add = _wkv_chunk_operators(*[cut(z) for z in (at, rt, bt, kt, bh, kh, vb)])
    S = [S_ref[j] for j in range(n_pairs)]
    for c in range(n_chunks):
        for j in range(n_pairs):
            i = c * n_pairs + j
            ls = slice(j * LANES, (j + 1) * LANES)
            ybuf_ref[c * C:(c + 1) * C, ls] = _pdot_nt(Q[i], S[j]) + y_add[i]
            S[j] = S[j] * decay_end[c * C:c * C + 1, ls] + _pdot(S[j], m_state[i]) + s_add[i]
    for j in range(n_pairs):
        S_ref[j] = S[j]

    y = ybuf_ref[...]
    inv_n = 1.0 / RWKV_HEAD
    mu = head_sum(y) * inv_n
    yc = y - mu
    var = head_sum(yc * yc) * inv_n
    y = yc * lax.rsqrt(var + RWKV_GN_EPS) * lng_ref[...] + lnb_ref[...]
    y = y + head_sum(r * k * rk_ref[...]) * v
    y_ref[...] = (y * g).astype(y_ref.dtype)


def _cast_plan(shape, n_steps):
    rows, cols = shape
    for ncb in (1, 2, 4, 8, 16):
        rb = n_steps // ncb
        if (n_steps % ncb == 0 and rows % rb == 0 and cols % ncb == 0
                and (rows // rb) % BF16_SUBLANES == 0 and (cols // ncb) % LANES == 0):
            return rb, ncb
    return None


def _rwkv_mixer(proj, G, small_w, small_off, p, v_first, to_cast):
    T = proj.shape[0]
    has_vres = v_first is not None
    tb = _pick(T, (256, 128, 64))
    bw = _pick(G, (RWKV_BLOCK_LANES, 2 * LANES, LANES))
    cb = G // bw
    dl, al, gl = p['decay_up'].shape[0], p['iclr_up'].shape[0], p['gate_up'].shape[0]
    ml = p['vres_up'].shape[0] if has_vres else 0
    mu = p['shift_mu']
    mu_r, mu_k, mu_v = (mu[q * G:(q + 1) * G].reshape(1, G) for q in range(3))
    mu_s = jnp.pad(mu[3 * G:], (0, small_w - (dl + al + gl + ml))).reshape(1, small_w)

    def colblk(first):
        return pl.BlockSpec((tb, bw), lambda q, t, f=first: (t, f + q))

    def vec(arr):
        return arr.reshape(1, G), pl.BlockSpec((1, bw), lambda q, t: (0, q))

    def up(arr):
        return arr.astype(BF16), pl.BlockSpec((arr.shape[0], bw), lambda q, t: (0, q))

    small_spec = pl.BlockSpec((tb, small_w), lambda q, t: (t, small_off // small_w))
    pair_spec = pl.BlockSpec((tb, bw), lambda q, t: (t, q))
    mu_spec = pl.BlockSpec((1, bw), lambda q, t: (0, q))
    args = [proj, proj, proj, proj]
    specs = [colblk(3 * cb), colblk(4 * cb), colblk(5 * cb), small_spec]
    if has_vres:
        args.append(v_first)
        specs.append(pair_spec)
    args += [mu_r, mu_k, mu_v, mu_s]
    specs += [mu_spec, mu_spec, mu_spec, pl.BlockSpec((1, small_w), lambda q, t: (0, 0))]
    names = ['decay_w0', 'decay_up', 'iclr_a0', 'iclr_up', 'gate_up']
    if has_vres:
        names += ['vres_v0', 'vres_up']
    names += ['k_k', 'k_a', 'r_k', 'lnx_g', 'lnx_b']
    for nm in names:
        arr, spec = up(p[nm]) if nm.endswith('_up') else vec(p[nm])
        args.append(arr)
        specs.append(spec)
    out_shape = [jax.ShapeDtypeStruct((T, G), BF16)]
    out_specs = [pair_spec]
    if not has_vres:
        out_shape.append(jax.ShapeDtypeStruct((T, G), F32))
        out_specs.append(pair_spec)
    n_t = T // tb
    n_steps = (G // bw) * n_t
    cast_names, cast_out = [], {}
    for nm, arr in to_cast.items():
        plan = _cast_plan(arr.shape, n_steps)
        if plan is None:
            cast_out[nm] = arr.astype(BF16)
            continue
        rb, ncb = plan
        spec = pl.BlockSpec((arr.shape[0] // rb, arr.shape[1] // ncb),
                            lambda q, t, ncb=ncb: ((q * n_t + t) // ncb, (q * n_t + t) % ncb))
        cast_names.append(nm)
        args.append(arr)
        specs.append(spec)
        out_shape.append(jax.ShapeDtypeStruct(arr.shape, BF16))
        out_specs.append(spec)
    n_main_out = 1 if has_vres else 2
    outs = pl.pallas_call(
        functools.partial(_rwkv_kernel, tb=tb, has_vres=has_vres, n_cast=len(cast_names)),
        grid=(G // bw, T // tb),
        in_specs=specs,
        out_specs=out_specs,
        out_shape=out_shape,
        scratch_shapes=[pltpu.VMEM((3, 8, bw), F32), pltpu.VMEM((8, small_w), F32),
                        pltpu.VMEM((bw // LANES, RWKV_HEAD, LANES), F32),
                        pltpu.VMEM((tb, bw), F32)],
        compiler_params=_cparams(("arbitrary", "arbitrary")),
        name="rwkv7_mixer",
    )(*args)
    cast_out.update(zip(cast_names, outs[n_main_out:]))
    return outs[0], (v_first if has_vres else outs[1]), cast_out


def _outproj_kernel(y0_ref, y1_ref, y2_ref, y3_ref, w_ref, x_ref, g_ref, b_ref, o_ref, ycat_ref, *, tn):
    G = y0_ref.shape[1]
    for j, y_ref in enumerate((y0_ref, y1_ref, y2_ref, y3_ref)):
        ycat_ref[:, j * G:(j + 1) * G] = y_ref[...]
    ycat = ycat_ref[...]
    for n in range(0, o_ref.shape[1], tn):
        o_ref[:, n:n + tn] = (DEEPNORM_ALPHA * x_ref[:, n:n + tn]
                              + jnp.dot(ycat, w_ref[:, n:n + tn], preferred_element_type=F32))
    o_ref[...] = _layer_norm(o_ref[...], g_ref[...], b_ref[...], LN_EPS)


def _outproj_ln(ys, w, x, g, b):
    T, D = x.shape
    G = ys[0].shape[1]
    tm = _pick(T, (256, 128))
    yspec = pl.BlockSpec((tm, G), lambda i: (i, 0))
    row = pl.BlockSpec((tm, D), lambda i: (i, 0))
    vec = pl.BlockSpec((1, D), lambda i: (0, 0))
    wspec = pl.BlockSpec(w.shape, lambda i: (0, 0), pipeline_mode=pl.Buffered(1))
    return pl.pallas_call(
        functools.partial(_outproj_kernel, tn=_pick(D, (512, 256, 128))),
        grid=(T // tm,),
        in_specs=[yspec] * 4 + [wspec, row, vec, vec],
        out_specs=row,
        out_shape=jax.ShapeDtypeStruct((T, D), F32),
        scratch_shapes=[pltpu.VMEM((tm, len(ys) * G), BF16)],
        compiler_params=_cparams(("arbitrary",)),
        name="outproj_ln",
    )(*ys, w, x, g.reshape(1, D), b.reshape(1, D))


def _ffn_kernel(x_hbm, wg_ref, wu_ref, wd_ref, g_ref, b_ref, o_hbm, acc_ref, xb_ref, in_sem, out_sem, *, tn):
    i, ni = pl.program_id(0), pl.num_programs(0)
    f, nf = pl.program_id(1), pl.num_programs(1)
    tm = xb_ref.shape[0]
    tr = _pick(tm, (256, 128))
    slot = i % 2

    def rows(c):
        return pl.ds(pl.multiple_of(c * tr, tr), tr)

    def x_copy(blk, s):
        return pltpu.make_async_copy(x_hbm.at[pl.ds(blk * tm, tm), :], acc_ref.at[s], in_sem.at[0])

    def out_copy(blk, s):
        return pltpu.make_async_copy(acc_ref.at[s], o_hbm.at[pl.ds(blk * tm, tm), :], out_sem.at[s])

    @pl.when(f == 0)
    def _():
        @pl.when(i == 0)
        def _():
            x_copy(0, 0).start()

        x_copy(i, slot).wait()

        def body(c, carry):
            xr = acc_ref[slot, rows(c), :]
            xb_ref[rows(c), :] = xr.astype(BF16)
            acc_ref[slot, rows(c), :] = DEEPNORM_ALPHA * xr
            return carry
        lax.fori_loop(0, tm // tr, body, 0)

    xb = xb_ref[...]
    gate = jnp.dot(xb, wg_ref[...], preferred_element_type=F32)
    upv = jnp.dot(xb, wu_ref[...], preferred_element_type=F32)
    hid = (gate * _sigmoid(gate) * upv).astype(BF16)
    for n in range(0, acc_ref.shape[2], tn):
        acc_ref[slot, :, n:n + tn] += jnp.dot(hid, wd_ref[:, n:n + tn], preferred_element_type=F32)

    @pl.when(f == jnp.maximum(nf - 2, 0))
    def _():
        @pl.when(i >= 1)
        def _():
            out_copy(i - 1, 1 - slot).wait()

        @pl.when(i + 1 < ni)
        def _():
            x_copy(i + 1, 1 - slot).start()

    @pl.when(f == nf - 1)
    def _():
        def body(c, carry):
            acc_ref[slot, rows(c), :] = _layer_norm(acc_ref[slot, rows(c), :], g_ref[...], b_ref[...], LN_EPS)
            return carry
        lax.fori_loop(0, tm // tr, body, 0)
        out_copy(i, slot).start()

        @pl.when(i == ni - 1)
        def _():
            out_copy(i, slot).wait()


def _ffn_ln(x, wg, wu, wd, g, b):
    T, D = x.shape
    F = wg.shape[1]
    tm = _pick(T, (1024, 512, 256, 128))
    tf = _pick(F, (256, 128))
    vec = pl.BlockSpec((1, D), lambda i, f: (0, 0))
    return pl.pallas_call(
        functools.partial(_ffn_kernel, tn=_pick(D, (512, 256, 128))),
        grid=(T // tm, F // tf),
        in_specs=[pl.BlockSpec(memory_space=pl.ANY),
                  pl.BlockSpec((D, tf), lambda i, f: (0, f)), pl.BlockSpec((D, tf), lambda i, f: (0, f)),
                  pl.BlockSpec((tf, D), lambda i, f: (f, 0)), vec, vec],
        out_specs=pl.BlockSpec(memory_space=pl.ANY),
        out_shape=jax.ShapeDtypeStruct((T, D), F32),
        scratch_shapes=[pltpu.VMEM((2, tm, D), F32), pltpu.VMEM((tm, D), BF16),
                        pltpu.SemaphoreType.DMA((1,)), pltpu.SemaphoreType.DMA((2,))],
        compiler_params=_cparams(("arbitrary", "arbitrary")),
        name="ffn_ln",
    )(x, wg, wu, wd, g.reshape(1, D), b.reshape(1, D))


def _hybrid_layer(x, v_first, p, w_in_t, next_w_in):
    T, D = x.shape
    G = D // 4
    n_small = w_in_t.shape[0] - 9 * G
    small_w = _round_up(n_small, 2 * LANES)
    proj = _inproj(x, w_in_t, 6 * G, n_small, 9 * G + small_w)
    y_conv, y_sgu, y_pool = _local_mixers(proj, G, p)
    later = {nm: p[nm] for nm in ('w_out', 'ffn_gate', 'ffn_up', 'ffn_down')}
    if next_w_in is not None:
        later['w_in_next_t'] = next_w_in.T
    y_rwkv, v_first, wb = _rwkv_mixer(proj, G, small_w, 9 * G, p, v_first, later)
    x = _outproj_ln((y_conv, y_rwkv, y_sgu, y_pool), wb['w_out'], x, p['ln_mix_g'], p['ln_mix_b'])
    x = _ffn_ln(x, wb['ffn_gate'], wb['ffn_up'], wb['ffn_down'], p['ln_ffn_g'], p['ln_ffn_b'])
    return x, v_first, wb.get('w_in_next_t')


_NAMES_0 = ('w_in', 'conv_w', 'shift_mu', 'decay_w0', 'decay_up', 'iclr_a0', 'iclr_up', 'gate_up',
            'k_k', 'k_a', 'r_k', 'lnx_g', 'lnx_b', 'sgu_ln_g', 'sgu_ln_b', 'sgu_w', 'sgu_b',
            'pool_w', 'pool_scale', 'w_out', 'ln_mix_g', 'ln_mix_b', 'ffn_gate', 'ffn_up', 'ffn_down',
            'ln_ffn_g', 'ln_ffn_b')
_NAMES_1 = _NAMES_0[:7] + ('vres_v0', 'vres_up') + _NAMES_0[7:]


def kernel(x, w_in_0, conv_w_0, shift_mu_0, decay_w0_0, decay_up_0, iclr_a0_0, iclr_up_0, gate_up_0, k_k_0, k_a_0, r_k_0, lnx_g_0, lnx_b_0, sgu_ln_g_0, sgu_ln_b_0, sgu_w_0, sgu_b_0, pool_w_0, pool_scale_0, w_out_0, ln_mix_g_0, ln_mix_b_0, ffn_gate_0, ffn_up_0, ffn_down_0, ln_ffn_g_0, ln_ffn_b_0, w_in_1, conv_w_1, shift_mu_1, decay_w0_1, decay_up_1, iclr_a0_1, iclr_up_1, vres_v0_1, vres_up_1, gate_up_1, k_k_1, k_a_1, r_k_1, lnx_g_1, lnx_b_1, sgu_ln_g_1, sgu_ln_b_1, sgu_w_1, sgu_b_1, pool_w_1, pool_scale_1, w_out_1, ln_mix_g_1, ln_mix_b_1, ffn_gate_1, ffn_up_1, ffn_down_1, ln_ffn_g_1, ln_ffn_b_1):
    p0 = dict(zip(_NAMES_0, (w_in_0, conv_w_0, shift_mu_0, decay_w0_0, decay_up_0, iclr_a0_0, iclr_up_0, gate_up_0, k_k_0, k_a_0, r_k_0, lnx_g_0, lnx_b_0, sgu_ln_g_0, sgu_ln_b_0, sgu_w_0, sgu_b_0, pool_w_0, pool_scale_0, w_out_0, ln_mix_g_0, ln_mix_b_0, ffn_gate_0, ffn_up_0, ffn_down_0, ln_ffn_g_0, ln_ffn_b_0)))
    p1 = dict(zip(_NAMES_1, (w_in_1, conv_w_1, shift_mu_1, decay_w0_1, decay_up_1, iclr_a0_1, iclr_up_1, vres_v0_1, vres_up_1, gate_up_1, k_k_1, k_a_1, r_k_1, lnx_g_1, lnx_b_1, sgu_ln_g_1, sgu_ln_b_1, sgu_w_1, sgu_b_1, pool_w_1, pool_scale_1, w_out_1, ln_mix_g_1, ln_mix_b_1, ffn_gate_1, ffn_up_1, ffn_down_1, ln_ffn_g_1, ln_ffn_b_1)))
    B, T, D = x.shape
    assert B == 1
    h = x.reshape(T, D)
    h, v_first, w_in_t = _hybrid_layer(h, None, p0, w_in_0.T.astype(BF16), w_in_1)
    h, _, _ = _hybrid_layer(h, v_first, p1, w_in_t, None)
    return h.reshape(B, T, D)
```

```python
import functools

import jax
import jax.numpy as jnp
from jax import lax
from jax.experimental import pallas as pl
from jax.experimental.pallas import tpu as pltpu

LANES = 128
BF16_SUBLANES = 16
RWKV_HEAD = 64
SGU_CHUNK = 128
SGU_HEAD = 128
CONV_W = 3
POOL_WINDOWS = (2, 4, 8, 16)
POOL_HALO = 16
CONV_HALO = 8
LN_EPS = 1e-5
RWKV_GN_EPS = 64e-5
WKV_CHUNK = 64
RWKV_BLOCK_LANES = 1024
DEPTH = 2
DEEPNORM_ALPHA = (2 * DEPTH) ** 0.25
VMEM_LIMIT_BYTES = 60 * 1024 * 1024

F32 = jnp.float32
BF16 = jnp.bfloat16


def _cparams(sem):
    return pltpu.CompilerParams(dimension_semantics=sem, vmem_limit_bytes=VMEM_LIMIT_BYTES)


def _pick(n, prefs):
    for p in prefs:
        if n % p == 0:
            return p
    return n


def _round_up(n, m):
    return (n + m - 1) // m * m


def _bdot(a, b):
    return jnp.dot(a.astype(BF16), b.astype(BF16), preferred_element_type=F32)


def _layer_norm(x, g, b, eps):
    mu = jnp.mean(x, axis=-1, keepdims=True)
    xc = x - mu
    var = jnp.mean(xc * xc, axis=-1, keepdims=True)
    return xc * lax.rsqrt(var + eps) * g + b


def _sigmoid(x):
    return 1.0 / (1.0 + jnp.exp(-x))


def _gelu_tanh(x):
    c = 0.7978845608028654
    return 0.5 * x * (1.0 + jnp.tanh(c * (x + 0.044715 * (x * x * x))))


def _softplus(x):
    return jnp.maximum(x, 0.0) + jnp.log(1.0 + jnp.exp(-jnp.abs(x)))


def _inproj_kernel(x_hbm, wt_ref, o_ref, xs_ref, xb_ref, sem):
    i, ni = pl.program_id(0), pl.num_programs(0)
    tm = xb_ref.shape[0]

    def x_copy(blk):
        return pltpu.make_async_copy(x_hbm.at[pl.ds(blk * tm, tm), :], xs_ref, sem.at[0])

    @pl.when(pl.program_id(1) == 0)
    def _():
        @pl.when(i == 0)
        def _():
            x_copy(0).start()

        x_copy(i).wait()
        xb_ref[...] = xs_ref[...].astype(BF16)

        @pl.when(i + 1 < ni)
        def _():
            x_copy(i + 1).start()

    o_ref[...] = lax.dot_general(xb_ref[...], wt_ref[...].astype(BF16), (((1,), (1,)), ((), ())),
                                 preferred_element_type=F32)


def _inproj(x, wt, head, n_small, width):
    T, D = x.shape
    tm = _pick(T, (1024, 512, 256, 128))
    tail = width - (wt.shape[0] - n_small)
    tn = next(c for c in (512, 256, 128) if head % c == 0 and width % c == 0 and tail == c)
    n_head = head // tn
    n_blocks = width // tn

    assert n_small % BF16_SUBLANES == 0

    def row_start(i, j):
        moved = head + n_small + (j - n_head) * tn
        start = jnp.where(j < n_head, j * tn, jnp.where(j < n_blocks - 1, moved, head))
        return pl.multiple_of(start, BF16_SUBLANES), 0

    return pl.pallas_call(
        _inproj_kernel,
        grid=(T // tm, n_blocks),
        in_specs=[pl.BlockSpec(memory_space=pl.ANY),
                  pl.BlockSpec((pl.Element(tn), pl.Element(D)), row_start)],
        out_specs=pl.BlockSpec((tm, tn), lambda i, j: (i, j)),
        out_shape=jax.ShapeDtypeStruct((T, width), F32),
        scratch_shapes=[pltpu.VMEM((tm, D), F32), pltpu.VMEM((tm, D), BF16), pltpu.SemaphoreType.DMA((1,))],
        compiler_params=_cparams(("arbitrary", "arbitrary")),
        name="inproj",
    )(x, wt)


def _local_mixers_kernel(h_ref, bg_ref, cg_ref, su_ref, sv_ref, pz_ref,
                         convw_ref, lng_ref, lnb_ref, sguw_ref, sgubt_ref, poolw_ref, pools_ref,
                         yconv_ref, ysgu_ref, ypool_ref,
                         zbuf_ref, pbuf_ref, *, tb):
    i = pl.program_id(0)

    @pl.when(i == 0)
    def _():
        zbuf_ref[0:CONV_HALO, :] = jnp.zeros((CONV_HALO, zbuf_ref.shape[1]), F32)
        pbuf_ref[0:POOL_HALO, :] = jnp.zeros((POOL_HALO, pbuf_ref.shape[1]), F32)

    zbuf_ref[CONV_HALO:CONV_HALO + tb, :] = cg_ref[...] * h_ref[...]
    conv = zbuf_ref[CONV_HALO - (CONV_W - 1):CONV_HALO - (CONV_W - 1) + tb, :] * convw_ref[0:1, :]
    for j in range(1, CONV_W):
        off = CONV_HALO - (CONV_W - 1 - j)
        conv = conv + zbuf_ref[off:off + tb, :] * convw_ref[j:j + 1, :]
    yconv_ref[...] = (bg_ref[...] * conv).astype(yconv_ref.dtype)
    zbuf_ref[0:CONV_HALO, :] = zbuf_ref[tb:tb + CONV_HALO, :]

    u = _gelu_tanh(su_ref[...])
    v = _layer_norm(_gelu_tanh(sv_ref[...]), lng_ref[...], lnb_ref[...], LN_EPS).astype(BF16)
    n_heads = sguw_ref.shape[0]
    row = lax.broadcasted_iota(jnp.int32, (SGU_CHUNK, SGU_CHUNK), 0)
    col = lax.broadcasted_iota(jnp.int32, (SGU_CHUNK, SGU_CHUNK), 1)
    causal = col <= row
    for hd in range(n_heads):
        w_h = jnp.where(causal, sguw_ref[hd], 0.0).astype(BF16)
        bias = sgubt_ref[:, hd:hd + 1]
        cs = slice(hd * SGU_HEAD, (hd + 1) * SGU_HEAD)
        for c in range(tb // SGU_CHUNK):
            rs = slice(c * SGU_CHUNK, (c + 1) * SGU_CHUNK)
            s = jnp.dot(w_h, v[rs, cs], preferred_element_type=F32) + bias
            ysgu_ref[rs, cs] = (u[rs, cs] * s).astype(ysgu_ref.dtype)

    pbuf_ref[POOL_HALO:POOL_HALO + tb, :] = pz_ref[...]
    pg = poolw_ref.shape[1]
    t_glob = i * tb + lax.broadcasted_iota(jnp.int32, (tb, 1), 0)
    for gi, win in enumerate(POOL_WINDOWS):
        cs = slice(gi * pg, (gi + 1) * pg)
        z = pbuf_ref[POOL_HALO:POOL_HALO + tb, cs]
        acc = z
        for j in range(1, win):
            acc = acc + pbuf_ref[POOL_HALO - j:POOL_HALO - j + tb, cs]
        cnt = jnp.minimum(t_glob + 1, win).astype(F32)
        d = acc / cnt - z
        y = _bdot(d, poolw_ref[gi])
        ypool_ref[:, cs] = (y * pools_ref[:, cs]).astype(ypool_ref.dtype)
    pbuf_ref[0:POOL_HALO, :] = pbuf_ref[tb:tb + POOL_HALO, :]


def _local_mixers(proj, G, p):
    T = proj.shape[0]
    tb = _pick(T, (256, 128))
    n_sgu = G // SGU_HEAD
    col = lambda c: pl.BlockSpec((tb, G), lambda i, c=c: (i, c))
    full = lambda a: pl.BlockSpec(a.shape, lambda i, n=a.ndim: (0,) * n)
    convw = p['conv_w']
    lng = p['sgu_ln_g'].reshape(1, G)
    lnb = p['sgu_ln_b'].reshape(1, G)
    sguw = p['sgu_w']
    sgubt = p['sgu_b'].T
    poolw = p['pool_w'].astype(BF16)
    pools = p['pool_scale'].reshape(1, G)
    outs = pl.pallas_call(
        functools.partial(_local_mixers_kernel, tb=tb),
        grid=(T // tb,),
        in_specs=[col(0), col(1), col(2), col(6), col(7), col(8),
                  full(convw), full(lng), full(lnb), full(sguw), full(sgubt), full(poolw), full(pools)],
        out_specs=[pl.BlockSpec((tb, G), lambda i: (i, 0))] * 3,
        out_shape=[jax.ShapeDtypeStruct((T, G), BF16)] * 3,
        scratch_shapes=[pltpu.VMEM((tb + CONV_HALO, G), F32), pltpu.VMEM((tb + POOL_HALO, G), F32)],
        compiler_params=_cparams(("arbitrary",)),
        name="local_mixers",
    )(proj, proj, proj, proj, proj, proj, convw, lng, lnb, sguw, sgubt, poolw, pools)
    assert n_sgu == sguw.shape[0]
    return outs


def _each(fn, *lists):
    return [fn(*args) for args in zip(*lists)]


def _unit_lower_inverse(Ls, eye, row, col):
    blk16 = (row >> 4) == (col >> 4)
    blk32 = (row >> 5) == (col >> 5)
    C = row.shape[0]
    P = [jnp.where(blk16, L, 0.0) for L in Ls]
    T = [eye + p for p in P]
    P = _each(_pdot, P, P)
    for _ in range(2):
        TP = _each(lambda t, p: _pdot(jnp.concatenate([t, p], axis=0), p), T, P)
        T = _each(lambda t, z: t + z[:C], T, TP)
        P = [z[C:] for z in TP]
    T = _each(lambda t, p: t + _pdot(t, p), T, P)
    for off_diag in (blk32 & jnp.logical_not(blk16), jnp.logical_not(blk32)):
        X = _each(lambda L, t: _pdot(jnp.where(off_diag, L, 0.0), t), Ls, T)
        T = _each(lambda t, x: t + _pdot(t, x), T, X)
    return T


def _block_diag(x):
    x = x.astype(BF16)
    first = lax.broadcasted_iota(jnp.int32, x.shape, 1) < RWKV_HEAD
    zero = jnp.zeros_like(x)
    return jnp.concatenate([jnp.where(first, x, zero), jnp.where(first, zero, x)], axis=0)


def _fold_diag(x):
    first = lax.broadcasted_iota(jnp.int32, (RWKV_HEAD, LANES), 1) < RWKV_HEAD
    return jnp.where(first, x[:RWKV_HEAD], x[RWKV_HEAD:])


def _pdot(a, b):
    return jnp.dot(a.astype(BF16), _block_diag(b), preferred_element_type=F32)


def _pdot_nt(a, b):
    return lax.dot_general(a.astype(BF16), _block_diag(b), (((1,), (1,)), ((), ())), preferred_element_type=F32)


def _pdot_tn(a, b):
    return _fold_diag(_bdot_tn(a, b))


def _bdot_tn(a, b):
    return lax.dot_general(a.astype(BF16), b.astype(BF16), (((0,), (0,)), ((), ())), preferred_element_type=F32)


def _wkv_chunk_operators(at, rt, bt, kt, bh, kh, v):
    C = at[0].shape[0]
    assert C == RWKV_HEAD
    row = lax.broadcasted_iota(jnp.int32, (C, LANES), 0)
    col = lax.broadcasted_iota(jnp.int32, (C, LANES), 1) & (RWKV_HEAD - 1)
    incl = col <= row
    strict = col < row
    eye = jnp.where(row == col, 1.0, 0.0).astype(F32)
    lanes = lambda z: (z[:, :LANES], z[:, LANES:])
    ar = _each(lambda x, y: jnp.concatenate([x, y.astype(BF16)], axis=0), at, rt)
    p = _each(lambda x, b_, k_: lax.dot_general(
        x, jnp.concatenate([_block_diag(b_), _block_diag(k_)], axis=0), (((1,), (1,)), ((), ())),
        preferred_element_type=F32), ar, bt, kt)
    Lab = [jnp.where(strict, z[:C, :LANES], 0.0) for z in p]
    Mak = [jnp.where(strict, z[:C, LANES:], 0.0) for z in p]
    Mrb = [jnp.where(incl, z[C:, :LANES], 0.0).astype(BF16) for z in p]
    Mrk = [jnp.where(incl, z[C:, LANES:], 0.0) for z in p]
    mv = _each(lambda m0, m1, x: _pdot(jnp.concatenate([m0, m1], axis=0), x), Mak, Mrk, v)
    MakV = [z[:C] for z in mv]
    MrkV = [z[C:] for z in mv]
    KV = _each(_pdot_tn, v, kh)
    Tinv = [t.astype(BF16) for t in _unit_lower_inverse(Lab, eye, row, col)]
    pdot2 = lambda x, y0, y1: lanes(jnp.dot(
        x, jnp.concatenate([_block_diag(y0), _block_diag(y1)], axis=1), preferred_element_type=F32))
    wu = _each(pdot2, Tinv, at, MakV)
    W = [z[0].astype(BF16) for z in wu]
    U = [z[1].astype(BF16) for z in wu]
    qy = _each(pdot2, Mrb, W, U)
    Q = _each(lambda x, z: x + z[0], rt, qy)
    y_add = _each(lambda z, m: z[1] + m, qy, MrkV)
    ms = _each(lambda w, u, b_: _bdot_tn(jnp.concatenate([w, u], axis=1), b_), W, U, bh)
    m_state = [_fold_diag(z[:LANES]) for z in ms]
    s_add = _each(lambda z, kv: _fold_diag(z[LANES:]) + kv, ms, KV)
    return Q, y_add, m_state, s_add


def _rwkv_kernel(*refs, tb, has_vres, n_cast):
    n_in = 21 if has_vres else 18
    n_out = 1 if has_vres else 2
    ins, refs = refs[:n_in], refs[n_in:]
    cast_in, refs = refs[:n_cast], refs[n_cast:]
    outs, refs = refs[:n_out], refs[n_out:]
    cast_out, refs = refs[:n_cast], refs[n_cast:]
    prev_ref, prevs_ref, S_ref, ybuf_ref = refs
    if has_vres:
        (r_ref, k_ref, v_ref, sm_ref, vfirst_ref,
         mur_ref, muk_ref, muv_ref, mus_ref,
         w0_ref, wup_ref, a0_ref, aup_ref, gup_ref, v0_ref, vup_ref,
         kk_ref, ka_ref, rk_ref, lng_ref, lnb_ref) = ins
        (y_ref,) = outs
    else:
        (r_ref, k_ref, v_ref, sm_ref,
         mur_ref, muk_ref, muv_ref, mus_ref,
         w0_ref, wup_ref, a0_ref, aup_ref, gup_ref,
         kk_ref, ka_ref, rk_ref, lng_ref, lnb_ref) = ins
        y_ref, vfirst_out_ref = outs
    t = pl.program_id(1)

    for src_ref, dst_ref in zip(cast_in, cast_out):
        dst_ref[...] = src_ref[...].astype(BF16)

    @pl.when(t == 0)
    def _():
        prev_ref[...] = jnp.zeros(prev_ref.shape, F32)
        prevs_ref[...] = jnp.zeros(prevs_ref.shape, F32)
        S_ref[...] = jnp.zeros(S_ref.shape, F32)

    def shift_mix(raw, prev_row, mu):
        first = lax.broadcasted_iota(jnp.int32, raw.shape, 0) == 0
        sh = jnp.where(first, prev_row, pltpu.roll(raw, 1, 0))
        return raw + (sh - raw) * mu

    r_raw, k_raw, v_raw, sm_raw = r_ref[...], k_ref[...], v_ref[...], sm_ref[...]
    r = shift_mix(r_raw, prev_ref[0, 7:8, :], mur_ref[...])
    k = shift_mix(k_raw, prev_ref[1, 7:8, :], muk_ref[...])
    v = shift_mix(v_raw, prev_ref[2, 7:8, :], muv_ref[...])
    sm = shift_mix(sm_raw, prevs_ref[7:8, :], mus_ref[...])
    prev_ref[0] = r_raw[tb - 8:tb, :]
    prev_ref[1] = k_raw[tb - 8:tb, :]
    prev_ref[2] = v_raw[tb - 8:tb, :]
    prevs_ref[...] = sm_raw[tb - 8:tb, :]

    dl, al, gl = wup_ref.shape[0], aup_ref.shape[0], gup_ref.shape[0]
    wd = sm[:, 0:dl]
    ad = sm[:, dl:dl + al]
    gd = sm[:, dl + al:dl + al + gl]
    w = -_softplus(-(w0_ref[...] + _bdot(jnp.tanh(wd), wup_ref[...]))) - 0.5
    lw = -jnp.exp(w)
    a = _sigmoid(a0_ref[...] + _bdot(ad, aup_ref[...]))
    g = _bdot(_sigmoid(gd), gup_ref[...])
    if has_vres:
        ml = vup_ref.shape[0]
        vd = sm[:, dl + al + gl:dl + al + gl + ml]
        v = v + (vfirst_ref[...] - v) * _sigmoid(v0_ref[...] + _bdot(vd, vup_ref[...]))
    else:
        vfirst_out_ref[...] = v

    li = lax.broadcasted_iota(jnp.int32, (LANES, LANES), 0) // RWKV_HEAD
    lj = lax.broadcasted_iota(jnp.int32, (LANES, LANES), 1) // RWKV_HEAD
    head_ones = jnp.where(li == lj, 1.0, 0.0).astype(BF16)
    width = y_ref.shape[1]
    n_pairs = width // LANES

    def split3(z):
        hi = z.astype(BF16)
        r1 = z - hi.astype(F32)
        mid = r1.astype(BF16)
        return hi, mid, (r1 - mid.astype(F32)).astype(BF16)

    def head_sum(z):
        parts = split3(z)[:2]
        return jnp.concatenate(
            [sum(jnp.dot(q[:, j * LANES:(j + 1) * LANES], head_ones, preferred_element_type=F32) for q in parts)
             for j in range(n_pairs)], axis=1)

    kk = k * kk_ref[...]
    kk = kk / jnp.maximum(jnp.sqrt(head_sum(kk * kk)), 1e-12)
    k = k * (1.0 + (a - 1.0) * ka_ref[...])

    C = WKV_CHUNK
    n_chunks = tb // C
    ti = lax.broadcasted_iota(jnp.int32, (tb, tb), 0)
    tj = lax.broadcasted_iota(jnp.int32, (tb, tb), 1)
    tri = jnp.where((tj <= ti) & ((ti // C) == (tj // C)), 1.0, 0.0).astype(BF16)
    lc = sum(jnp.dot(tri, q, preferred_element_type=F32) for q in split3(lw))
    lc_end = jnp.concatenate(
        [jnp.broadcast_to(lc[(c + 1) * C - 1:(c + 1) * C, :], (C, width)) for c in range(n_chunks)], axis=0)
    e_neg = jnp.exp(-lc)
    e_end = jnp.exp(lc_end - lc)
    at = (-kk * jnp.exp(lc - lw)).astype(BF16)
    rt = r * jnp.exp(lc)
    bt = (kk * a * e_neg).astype(BF16)
    kt = (k * e_neg).astype(BF16)
    bh = (kk * a * e_end).astype(BF16)
    kh = (k * e_end).astype(BF16)
    vb = v.astype(BF16)
    decay_end = jnp.exp(lc_end)

    tiles = [(c, j) for c in range(n_chunks) for j in range(n_pairs)]
    cut = lambda z: [z[c * C:(c + 1) * C, j * LANES:(j + 1) * LANES] for c, j in tiles]
    Q, y_add, m_state, s_add = _wkv_chunk_operators(*[cut(z) for z in (at, rt, bt, kt, bh, kh, vb)])
    S = [S_ref[j] for j in range(n_pairs)]
    for c in range(n_chunks):
        for j in range(n_pairs):
            i = c * n_pairs + j
            ls = slice(j * LANES, (j + 1) * LANES)
            ybuf_ref[c * C:(c + 1) * C, ls] = _pdot_nt(Q[i], S[j]) + y_add[i]
            S[j] = S[j] * decay_end[c * C:c * C + 1, ls] + _pdot(S[j], m_state[i]) + s_add[i]
    for j in range(n_pairs):
        S_ref[j] = S[j]

    y = ybuf_ref[...]
    inv_n = 1.0 / RWKV_HEAD
    mu = head_sum(y) * inv_n
    yc = y - mu
    var = head_sum(yc * yc) * inv_n
    y = yc * lax.rsqrt(var + RWKV_GN_EPS) * lng_ref[...] + lnb_ref[...]
    y = y + head_sum(r * k * rk_ref[...]) * v
    y_ref[...] = (y * g).astype(y_ref.dtype)


def _cast_plan(shape, n_steps):
    rows, cols = shape
    for ncb in (1, 2, 4, 8, 16):
        rb = n_steps // ncb
        if (n_steps % ncb == 0 and rows % rb == 0 and cols % ncb == 0
                and (rows // rb) % BF16_SUBLANES == 0 and (cols // ncb) % LANES == 0):
            return rb, ncb
    return None


def _rwkv_mixer(proj, G, small_w, small_off, p, v_first, to_cast):
    T = proj.shape[0]
    has_vres = v_first is not None
    tb = _pick(T, (256, 128, 64))
    bw = _pick(G, (RWKV_BLOCK_LANES, 2 * LANES, LANES))
    cb = G // bw
    dl, al, gl = p['decay_up'].shape[0], p['iclr_up'].shape[0], p['gate_up'].shape[0]
    ml = p['vres_up'].shape[0] if has_vres else 0
    mu = p['shift_mu']
    mu_r, mu_k, mu_v = (mu[q * G:(q + 1) * G].reshape(1, G) for q in range(3))
    mu_s = jnp.pad(mu[3 * G:], (0, small_w - (dl + al + gl + ml))).reshape(1, small_w)

    def colblk(first):
        return pl.BlockSpec((tb, bw), lambda q, t, f=first: (t, f + q))

    def vec(arr):
        return arr.reshape(1, G), pl.BlockSpec((1, bw), lambda q, t: (0, q))

    def up(arr):
        return arr.astype(BF16), pl.BlockSpec((arr.shape[0], bw), lambda q, t: (0, q))

    small_spec = pl.BlockSpec((tb, small_w), lambda q, t: (t, small_off // small_w))
    pair_spec = pl.BlockSpec((tb, bw), lambda q, t: (t, q))
    mu_spec = pl.BlockSpec((1, bw), lambda q, t: (0, q))
    args = [proj, proj, proj, proj]
    specs = [colblk(3 * cb), colblk(4 * cb), colblk(5 * cb), small_spec]
    if has_vres:
        args.append(v_first)
        specs.append(pair_spec)
    args += [mu_r, mu_k, mu_v, mu_s]
    specs += [mu_spec, mu_spec, mu_spec, pl.BlockSpec((1, small_w), lambda q, t: (0, 0))]
    names = ['decay_w0', 'decay_up', 'iclr_a0', 'iclr_up', 'gate_up']
    if has_vres:
        names += ['vres_v0', 'vres_up']
    names += ['k_k', 'k_a', 'r_k', 'lnx_g', 'lnx_b']
    for nm in names:
        arr, spec = up(p[nm]) if nm.endswith('_up') else vec(p[nm])
        args.append(arr)
        specs.append(spec)
    out_shape = [jax.ShapeDtypeStruct((T, G), BF16)]
    out_specs = [pair_spec]
    if not has_vres:
        out_shape.append(jax.ShapeDtypeStruct((T, G), F32))
        out_specs.append(pair_spec)
    n_t = T // tb
    n_steps = (G // bw) * n_t
    cast_names, cast_out = [], {}
    for nm, arr in to_cast.items():
        plan = _cast_plan(arr.shape, n_steps)
        if plan is None:
            cast_out[nm] = arr.astype(BF16)
            continue
        rb, ncb = plan
        spec = pl.BlockSpec((arr.shape[0] // rb, arr.shape[1] // ncb),
                            lambda q, t, ncb=ncb: ((q * n_t + t) // ncb, (q * n_t + t) % ncb))
        cast_names.append(nm)
        args.append(arr)
        specs.append(spec)
        out_shape.append(jax.ShapeDtypeStruct(arr.shape, BF16))
        out_specs.append(spec)
    n_main_out = 1 if has_vres else 2
    outs = pl.pallas_call(
        functools.partial(_rwkv_kernel, tb=tb, has_vres=has_vres, n_cast=len(cast_names)),
        grid=(G // bw, T // tb),
        in_specs=specs,
        out_specs=out_specs,
        out_shape=out_shape,
        scratch_shapes=[pltpu.VMEM((3, 8, bw), F32), pltpu.VMEM((8, small_w), F32),
                        pltpu.VMEM((bw // LANES, RWKV_HEAD, LANES), F32),
                        pltpu.VMEM((tb, bw), F32)],
        compiler_params=_cparams(("arbitrary", "arbitrary")),
        name="rwkv7_mixer",
    )(*args)
    cast_out.update(zip(cast_names, outs[n_main_out:]))
    return outs[0], (v_first if has_vres else outs[1]), cast_out


def _outproj_kernel(y0_ref, y1_ref, y2_ref, y3_ref, w_ref, x_ref, g_ref, b_ref, o_ref, ycat_ref, *, tn):
    G = y0_ref.shape[1]
    for j, y_ref in enumerate((y0_ref, y1_ref, y2_ref, y3_ref)):
        ycat_ref[:, j * G:(j + 1) * G] = y_ref[...]
    ycat = ycat_ref[...]
    for n in range(0, o_ref.shape[1], tn):
        o_ref[:, n:n + tn] = (DEEPNORM_ALPHA * x_ref[:, n:n + tn]
                              + jnp.dot(ycat, w_ref[:, n:n + tn], preferred_element_type=F32))
    o_ref[...] = _layer_norm(o_ref[...], g_ref[...], b_ref[...], LN_EPS)


def _outproj_ln(ys, w, x, g, b):
    T, D = x.shape
    G = ys[0].shape[1]
    tm = _pick(T, (256, 128))
    yspec = pl.BlockSpec((tm, G), lambda i: (i, 0))
    row = pl.BlockSpec((tm, D), lambda i: (i, 0))
    vec = pl.BlockSpec((1, D), lambda i: (0, 0))
    wspec = pl.BlockSpec(w.shape, lambda i: (0, 0), pipeline_mode=pl.Buffered(1))
    return pl.pallas_call(
        functools.partial(_outproj_kernel, tn=_pick(D, (512, 256, 128))),
        grid=(T // tm,),
        in_specs=[yspec] * 4 + [wspec, row, vec, vec],
        out_specs=row,
        out_shape=jax.ShapeDtypeStruct((T, D), F32),
        scratch_shapes=[pltpu.VMEM((tm, len(ys) * G), BF16)],
        compiler_params=_cparams(("arbitrary",)),
        name="outproj_ln",
    )(*ys, w, x, g.reshape(1, D), b.reshape(1, D))


def _ffn_kernel(x_hbm, wg_ref, wu_ref, wd_ref, g_ref, b_ref, o_hbm, acc_ref, xb_ref, in_sem, out_sem, *, tn):
    i, ni = pl.program_id(0), pl.num_programs(0)
    f, nf = pl.program_id(1), pl.num_programs(1)
    tm = xb_ref.shape[0]
    tr = _pick(tm, (256, 128))
    slot = i % 2

    def rows(c):
        return pl.ds(pl.multiple_of(c * tr, tr), tr)

    def x_copy(blk, s):
        return pltpu.make_async_copy(x_hbm.at[pl.ds(blk * tm, tm), :], acc_ref.at[s], in_sem.at[0])

    def out_copy(blk, s):
        return pltpu.make_async_copy(acc_ref.at[s], o_hbm.at[pl.ds(blk * tm, tm), :], out_sem.at[s])

    @pl.when(f == 0)
    def _():
        @pl.when(i == 0)
        def _():
            x_copy(0, 0).start()

        x_copy(i, slot).wait()

        def body(c, carry):
            xr = acc_ref[slot, rows(c), :]
            xb_ref[rows(c), :] = xr.astype(BF16)
            acc_ref[slot, rows(c), :] = DEEPNORM_ALPHA * xr
            return carry
        lax.fori_loop(0, tm // tr, body, 0)

    xb = xb_ref[...]
    gate = jnp.dot(xb, wg_ref[...], preferred_element_type=F32)
    upv = jnp.dot(xb, wu_ref[...], preferred_element_type=F32)
    hid = (gate * _sigmoid(gate) * upv).astype(BF16)
    for n in range(0, acc_ref.shape[2], tn):
        acc_ref[slot, :, n:n + tn] += jnp.dot(hid, wd_ref[:, n:n + tn], preferred_element_type=F32)

    @pl.when(f == jnp.maximum(nf - 2, 0))
    def _():
        @pl.when(i >= 1)
        def _():
            out_copy(i - 1, 1 - slot).wait()

        @pl.when(i + 1 < ni)
        def _():
            x_copy(i + 1, 1 - slot).start()

    @pl.when(f == nf - 1)
    def _():
        def body(c, carry):
            acc_ref[slot, rows(c), :] = _layer_norm(acc_ref[slot, rows(c), :], g_ref[...], b_ref[...], LN_EPS)
            return carry
        lax.fori_loop(0, tm // tr, body, 0)
        out_copy(i, slot).start()

        @pl.when(i == ni - 1)
        def _():
            out_copy(i, slot).wait()


def _ffn_ln(x, wg, wu, wd, g, b):
    T, D = x.shape
    F = wg.shape[1]
    tm = _pick(T, (1024, 512, 256, 128))
    tf = _pick(F, (256, 128))
    vec = pl.BlockSpec((1, D), lambda i, f: (0, 0))
    return pl.pallas_call(
        functools.partial(_ffn_kernel, tn=_pick(D, (512, 256, 128))),
        grid=(T // tm, F // tf),
        in_specs=[pl.BlockSpec(memory_space=pl.ANY),
                  pl.BlockSpec((D, tf), lambda i, f: (0, f)), pl.BlockSpec((D, tf), lambda i, f: (0, f)),
                  pl.BlockSpec((tf, D), lambda i, f: (f, 0)), vec, vec],
        out_specs=pl.BlockSpec(memory_space=pl.ANY),
        out_shape=jax.ShapeDtypeStruct((T, D), F32),
        scratch_shapes=[pltpu.VMEM((2, tm, D), F32), pltpu.VMEM((tm, D), BF16),
                        pltpu.SemaphoreType.DMA((1,)), pltpu.SemaphoreType.DMA((2,))],
        compiler_params=_cparams(("arbitrary", "arbitrary")),
        name="ffn_ln",
    )(x, wg, wu, wd, g.reshape(1, D), b.reshape(1, D))


def _hybrid_layer(x, v_first, p, w_in_t, next_w_in):
    T, D = x.shape
    G = D // 4
    n_small = w_in_t.shape[0] - 9 * G
    small_w = _round_up(n_small, 2 * LANES)
    proj = _inproj(x, w_in_t, 6 * G, n_small, 9 * G + small_w)
    y_conv, y_sgu, y_pool = _local_mixers(proj, G, p)
    later = {nm: p[nm] for nm in ('w_out', 'ffn_gate', 'ffn_up', 'ffn_down')}
    if next_w_in is not None:
        later['w_in_next_t'] = next_w_in.T
    y_rwkv, v_first, wb = _rwkv_mixer(proj, G, small_w, 9 * G, p, v_first, later)
    x = _outproj_ln((y_conv, y_rwkv, y_sgu, y_pool), wb['w_out'], x, p['ln_mix_g'], p['ln_mix_b'])
    x = _ffn_ln(x, wb['ffn_gate'], wb['ffn_up'], wb['ffn_down'], p['ln_ffn_g'], p['ln_ffn_b'])
    return x, v_first, wb.get('w_in_next_t')


_NAMES_0 = ('w_in', 'conv_w', 'shift_mu', 'decay_w0', 'decay_up', 'iclr_a0', 'iclr_up', 'gate_up',
            'k_k', 'k_a', 'r_k', 'lnx_g', 'lnx_b', 'sgu_ln_g', 'sgu_ln_b', 'sgu_w', 'sgu_b',
            'pool_w', 'pool_scale', 'w_out', 'ln_mix_g', 'ln_mix_b', 'ffn_gate', 'ffn_up', 'ffn_down',
            'ln_ffn_g', 'ln_ffn_b')
_NAMES_1 = _NAMES_0[:7] + ('vres_v0', 'vres_up') + _NAMES_0[7:]


def kernel(x, w_in_0, conv_w_0, shift_mu_0, decay_w0_0, decay_up_0, iclr_a0_0, iclr_up_0, gate_up_0, k_k_0, k_a_0, r_k_0, lnx_g_0, lnx_b_0, sgu_ln_g_0, sgu_ln_b_0, sgu_w_0, sgu_b_0, pool_w_0, pool_scale_0, w_out_0, ln_mix_g_0, ln_mix_b_0, ffn_gate_0, ffn_up_0, ffn_down_0, ln_ffn_g_0, ln_ffn_b_0, w_in_1, conv_w_1, shift_mu_1, decay_w0_1, decay_up_1, iclr_a0_1, iclr_up_1, vres_v0_1, vres_up_1, gate_up_1, k_k_1, k_a_1, r_k_1, lnx_g_1, lnx_b_1, sgu_ln_g_1, sgu_ln_b_1, sgu_w_1, sgu_b_1, pool_w_1, pool_scale_1, w_out_1, ln_mix_g_1, ln_mix_b_1, ffn_gate_1, ffn_up_1, ffn_down_1, ln_ffn_g_1, ln_ffn_b_1):
    p0 = dict(zip(_NAMES_0, (w_in_0, conv_w_0, shift_mu_0, decay_w0_0, decay_up_0, iclr_a0_0, iclr_up_0, gate_up_0, k_k_0, k_a_0, r_k_0, lnx_g_0, lnx_b_0, sgu_ln_g_0, sgu_ln_b_0, sgu_w_0, sgu_b_0, pool_w_0, pool_scale_0, w_out_0, ln_mix_g_0, ln_mix_b_0, ffn_gate_0, ffn_up_0, ffn_down_0, ln_ffn_g_0, ln_ffn_b_0)))
    p1 = dict(zip(_NAMES_1, (w_in_1, conv_w_1, shift_mu_1, decay_w0_1, decay_up_1, iclr_a0_1, iclr_up_1, vres_v0_1, vres_up_1, gate_up_1, k_k_1, k_a_1, r_k_1, lnx_g_1, lnx_b_1, sgu_ln_g_1, sgu_ln_b_1, sgu_w_1, sgu_b_1, pool_w_1, pool_scale_1, w_out_1, ln_mix_g_1, ln_mix_b_1, ffn_gate_1, ffn_up_1, ffn_down_1, ln_ffn_g_1, ln_ffn_b_1)))
    B, T, D = x.shape
    assert B == 1
    h = x.reshape(T, D)
    h, v_first, w_in_t = _hybrid_layer(h, None, p0, w_in_0.T, w_in_1)
    h, _, _ = _hybrid_layer(h, v_first, p1, w_in_t, None)
    return h.reshape(B, T, D)
```

```python
import functools

import jax
import jax.numpy as jnp
from jax import lax
from jax.experimental import pallas as pl
from jax.experimental.pallas import tpu as pltpu

LANES = 128
BF16_SUBLANES = 16
RWKV_HEAD = 64
SGU_CHUNK = 128
SGU_HEAD = 128
CONV_W = 3
POOL_WINDOWS = (2, 4, 8, 16)
POOL_HALO = 16
CONV_HALO = 8
LN_EPS = 1e-5
RWKV_GN_EPS = 64e-5
WKV_CHUNK = 64
RWKV_BLOCK_LANES = 1024
DEPTH = 2
DEEPNORM_ALPHA = (2 * DEPTH) ** 0.25
VMEM_LIMIT_BYTES = 60 * 1024 * 1024

F32 = jnp.float32
BF16 = jnp.bfloat16


def _cparams(sem):
    return pltpu.CompilerParams(dimension_semantics=sem, vmem_limit_bytes=VMEM_LIMIT_BYTES)


def _pick(n, prefs):
    for p in prefs:
        if n % p == 0:
            return p
    return n


def _round_up(n, m):
    return (n + m - 1) // m * m


def _bdot(a, b):
    return jnp.dot(a.astype(BF16), b.astype(BF16), preferred_element_type=F32)


def _layer_norm(x, g, b, eps):
    mu = jnp.mean(x, axis=-1, keepdims=True)
    xc = x - mu
    var = jnp.mean(xc * xc, axis=-1, keepdims=True)
    return xc * lax.rsqrt(var + eps) * g + b


def _sigmoid(x):
    return 1.0 / (1.0 + jnp.exp(-x))


def _gelu_tanh(x):
    c = 0.7978845608028654
    return 0.5 * x * (1.0 + jnp.tanh(c * (x + 0.044715 * (x * x * x))))


def _softplus(x):
    return jnp.maximum(x, 0.0) + jnp.log(1.0 + jnp.exp(-jnp.abs(x)))


def _inproj_kernel(x_hbm, wt_ref, *rest, cast_blocks):
    n_cast = len(cast_blocks)
    cast_in, o_ref, cast_out = rest[:n_cast], rest[n_cast], rest[n_cast + 1:2 * n_cast + 1]
    xs_ref, xb_ref, sem = rest[2 * n_cast + 1:]
    i, ni = pl.program_id(0), pl.num_programs(0)
    tm = xb_ref.shape[0]

    def x_copy(blk):
        return pltpu.make_async_copy(x_hbm.at[pl.ds(blk * tm, tm), :], xs_ref, sem.at[0])

    @pl.when(pl.program_id(1) == 0)
    def _():
        @pl.when(i == 0)
        def _():
            x_copy(0).start()

        x_copy(i).wait()
        xb_ref[...] = xs_ref[...].astype(BF16)

        @pl.when(i + 1 < ni)
        def _():
            x_copy(i + 1).start()

    o_ref[...] = lax.dot_general(xb_ref[...], wt_ref[...].astype(BF16), (((1,), (1,)), ((), ())),
                                 preferred_element_type=F32)

    for src_ref, dst_ref in zip(cast_in, cast_out):
        dst_ref[...] = src_ref[...].astype(BF16)


def _cast_blocks(shape, max_steps):
    rows, cols = shape
    for n in (256, 128, 64, 32, 16, 8, 4, 2, 1):
        for ncb in (1, 2, 4, 8, 16, 32, 64):
            rb = n // ncb
            if (n <= max_steps and n % ncb == 0 and rows % rb == 0 and cols % ncb == 0
                    and (rows // rb) % BF16_SUBLANES == 0 and (cols // ncb) % LANES == 0):
                return rb, ncb
    return None


def _inproj(x, wt, head, n_small, width, to_cast):
    T, D = x.shape
    tm = _pick(T, (1024, 512, 256, 128))
    tail = width - (wt.shape[0] - n_small)
    tn = next(c for c in (512, 256, 128) if head % c == 0 and width % c == 0 and tail == c)
    n_head = head // tn
    n_blocks = width // tn

    assert n_small % BF16_SUBLANES == 0

    def row_start(i, j):
        moved = head + n_small + (j - n_head) * tn
        start = jnp.where(j < n_head, j * tn, jnp.where(j < n_blocks - 1, moved, head))
        return pl.multiple_of(start, BF16_SUBLANES), 0

    args = [x, wt]
    in_specs = [pl.BlockSpec(memory_space=pl.ANY), pl.BlockSpec((pl.Element(tn), pl.Element(D)), row_start)]
    out_shape = [jax.ShapeDtypeStruct((T, width), F32)]
    out_specs = [pl.BlockSpec((tm, tn), lambda i, j: (i, j))]
    cast_names, cast_blocks, cast_out = [], [], {}
    for nm, arr in to_cast.items():
        plan = _cast_blocks(arr.shape, (T // tm) * n_blocks)
        if plan is None:
            cast_out[nm] = arr.astype(BF16)
            continue
        rb, ncb = plan

        def blk(i, j, n=rb * ncb, ncb=ncb):
            s = jnp.minimum(i * n_blocks + j, n - 1)
            return s // ncb, s % ncb

        spec = pl.BlockSpec((arr.shape[0] // rb, arr.shape[1] // ncb), blk)
        cast_names.append(nm)
        cast_blocks.append(rb * ncb)
        args.append(arr)
        in_specs.append(spec)
        out_shape.append(jax.ShapeDtypeStruct(arr.shape, BF16))
        out_specs.append(spec)
    outs = pl.pallas_call(
        functools.partial(_inproj_kernel, cast_blocks=tuple(cast_blocks)),
        grid=(T // tm, n_blocks),
        in_specs=in_specs,
        out_specs=out_specs,
        out_shape=out_shape,
        scratch_shapes=[pltpu.VMEM((tm, D), F32), pltpu.VMEM((tm, D), BF16), pltpu.SemaphoreType.DMA((1,))],
        compiler_params=_cparams(("arbitrary", "arbitrary")),
        name="inproj",
    )(*args)
    cast_out.update(zip(cast_names, outs[1:]))
    return outs[0], cast_out


def _local_mixers_kernel(h_ref, bg_ref, cg_ref, su_ref, sv_ref, pz_ref,
                         convw_ref, lng_ref, lnb_ref, sguw_ref, sgubt_ref, poolw_ref, pools_ref,
                         yconv_ref, ysgu_ref, ypool_ref,
                         zbuf_ref, pbuf_ref, *, tb):
    i = pl.program_id(0)

    @pl.when(i == 0)
    def _():
        zbuf_ref[0:CONV_HALO, :] = jnp.zeros((CONV_HALO, zbuf_ref.shape[1]), F32)
        pbuf_ref[0:POOL_HALO, :] = jnp.zeros((POOL_HALO, pbuf_ref.shape[1]), F32)

    zbuf_ref[CONV_HALO:CONV_HALO + tb, :] = cg_ref[...] * h_ref[...]
    conv = zbuf_ref[CONV_HALO - (CONV_W - 1):CONV_HALO - (CONV_W - 1) + tb, :] * convw_ref[0:1, :]
    for j in range(1, CONV_W):
        off = CONV_HALO - (CONV_W - 1 - j)
        conv = conv + zbuf_ref[off:off + tb, :] * convw_ref[j:j + 1, :]
    yconv_ref[...] = (bg_ref[...] * conv).astype(yconv_ref.dtype)
    zbuf_ref[0:CONV_HALO, :] = zbuf_ref[tb:tb + CONV_HALO, :]

    u = _gelu_tanh(su_ref[...])
    v = _layer_norm(_gelu_tanh(sv_ref[...]), lng_ref[...], lnb_ref[...], LN_EPS).astype(BF16)
    n_heads = sguw_ref.shape[0]
    row = lax.broadcasted_iota(jnp.int32, (SGU_CHUNK, SGU_CHUNK), 0)
    col = lax.broadcasted_iota(jnp.int32, (SGU_CHUNK, SGU_CHUNK), 1)
    causal = col <= row
    for hd in range(n_heads):
        w_h = jnp.where(causal, sguw_ref[hd], 0.0).astype(BF16)
        bias = sgubt_ref[:, hd:hd + 1]
        cs = slice(hd * SGU_HEAD, (hd + 1) * SGU_HEAD)
        for c in range(tb // SGU_CHUNK):
            rs = slice(c * SGU_CHUNK, (c + 1) * SGU_CHUNK)
            s = jnp.dot(w_h, v[rs, cs], preferred_element_type=F32) + bias
            ysgu_ref[rs, cs] = (u[rs, cs] * s).astype(ysgu_ref.dtype)

    pbuf_ref[POOL_HALO:POOL_HALO + tb, :] = pz_ref[...]
    pg = poolw_ref.shape[1]
    t_glob = i * tb + lax.broadcasted_iota(jnp.int32, (tb, 1), 0)
    for gi, win in enumerate(POOL_WINDOWS):
        cs = slice(gi * pg, (gi + 1) * pg)
        z = pbuf_ref[POOL_HALO:POOL_HALO + tb, cs]
        acc = z
        for j in range(1, win):
            acc = acc + pbuf_ref[POOL_HALO - j:POOL_HALO - j + tb, cs]
        cnt = jnp.minimum(t_glob + 1, win).astype(F32)
        d = acc / cnt - z
        y = _bdot(d, poolw_ref[gi])
        ypool_ref[:, cs] = (y * pools_ref[:, cs]).astype(ypool_ref.dtype)
    pbuf_ref[0:POOL_HALO, :] = pbuf_ref[tb:tb + POOL_HALO, :]


def _local_mixers(proj, G, p):
    T = proj.shape[0]
    tb = _pick(T, (256, 128))
    n_sgu = G // SGU_HEAD
    col = lambda c: pl.BlockSpec((tb, G), lambda i, c=c: (i, c))
    full = lambda a: pl.BlockSpec(a.shape, lambda i, n=a.ndim: (0,) * n)
    convw = p['conv_w']
    lng = p['sgu_ln_g'].reshape(1, G)
    lnb = p['sgu_ln_b'].reshape(1, G)
    sguw = p['sgu_w']
    sgubt = p['sgu_b'].T
    poolw = p['pool_w'].astype(BF16)
    pools = p['pool_scale'].reshape(1, G)
    outs = pl.pallas_call(
        functools.partial(_local_mixers_kernel, tb=tb),
        grid=(T // tb,),
        in_specs=[col(0), col(1), col(2), col(6), col(7), col(8),
                  full(convw), full(lng), full(lnb), full(sguw), full(sgubt), full(poolw), full(pools)],
        out_specs=[pl.BlockSpec((tb, G), lambda i: (i, 0))] * 3,
        out_shape=[jax.ShapeDtypeStruct((T, G), BF16)] * 3,
        scratch_shapes=[pltpu.VMEM((tb + CONV_HALO, G), F32), pltpu.VMEM((tb + POOL_HALO, G), F32)],
        compiler_params=_cparams(("arbitrary",)),
        name="local_mixers",
    )(proj, proj, proj, proj, proj, proj, convw, lng, lnb, sguw, sgubt, poolw, pools)
    assert n_sgu == sguw.shape[0]
    return outs


def _each(fn, *lists):
    return [fn(*args) for args in zip(*lists)]


def _unit_lower_inverse(Ls, eye, row, col):
    blk16 = (row >> 4) == (col >> 4)
    blk32 = (row >> 5) == (col >> 5)
    C = row.shape[0]
    P = [jnp.where(blk16, L, 0.0) for L in Ls]
    T = [eye + p for p in P]
    P = _each(_pdot, P, P)
    for _ in range(2):
        TP = _each(lambda t, p: _pdot(jnp.concatenate([t, p], axis=0), p), T, P)
        T = _each(lambda t, z: t + z[:C], T, TP)
        P = [z[C:] for z in TP]
    T = _each(lambda t, p: t + _pdot(t, p), T, P)
    for off_diag in (blk32 & jnp.logical_not(blk16), jnp.logical_not(blk32)):
        X = _each(lambda L, t: _pdot(jnp.where(off_diag, L, 0.0), t), Ls, T)
        T = _each(lambda t, x: t + _pdot(t, x), T, X)
    return T


def _block_diag(x):
    x = x.astype(BF16)
    first = lax.broadcasted_iota(jnp.int32, x.shape, 1) < RWKV_HEAD
    zero = jnp.zeros_like(x)
    return jnp.concatenate([jnp.where(first, x, zero), jnp.where(first, zero, x)], axis=0)


def _fold_diag(x):
    first = lax.broadcasted_iota(jnp.int32, (RWKV_HEAD, LANES), 1) < RWKV_HEAD
    return jnp.where(first, x[:RWKV_HEAD], x[RWKV_HEAD:])


def _pdot(a, b):
    return jnp.dot(a.astype(BF16), _block_diag(b), preferred_element_type=F32)


def _pdot_nt(a, b):
    return lax.dot_general(a.astype(BF16), _block_diag(b), (((1,), (1,)), ((), ())), preferred_element_type=F32)


def _pdot_tn(a, b):
    return _fold_diag(_bdot_tn(a, b))


def _bdot_tn(a, b):
    return lax.dot_general(a.astype(BF16), b.astype(BF16), (((0,), (0,)), ((), ())), preferred_element_type=F32)


def _wkv_chunk_operators(at, rt, bt, kt, bh, kh, v):
    C = at[0].shape[0]
    assert C == RWKV_HEAD
    row = lax.broadcasted_iota(jnp.int32, (C, LANES), 0)
    col = lax.broadcasted_iota(jnp.int32, (C, LANES), 1) & (RWKV_HEAD - 1)
    incl = col <= row
    strict = col < row
    eye = jnp.where(row == col, 1.0, 0.0).astype(F32)
    lanes = lambda z: (z[:, :LANES], z[:, LANES:])
    ar = _each(lambda x, y: jnp.concatenate([x, y.astype(BF16)], axis=0), at, rt)
    p = _each(lambda x, b_, k_: lax.dot_general(
        x, jnp.concatenate([_block_diag(b_), _block_diag(k_)], axis=0), (((1,), (1,)), ((), ())),
        preferred_element_type=F32), ar, bt, kt)
    Lab = [jnp.where(strict, z[:C, :LANES], 0.0) for z in p]
    Mak = [jnp.where(strict, z[:C, LANES:], 0.0) for z in p]
    Mrb = [jnp.where(incl, z[C:, :LANES], 0.0).astype(BF16) for z in p]
    Mrk = [jnp.where(incl, z[C:, LANES:], 0.0) for z in p]
    mv = _each(lambda m0, m1, x: _pdot(jnp.concatenate([m0, m1], axis=0), x), Mak, Mrk, v)
    MakV = [z[:C] for z in mv]
    MrkV = [z[C:] for z in mv]
    KV = _each(_pdot_tn, v, kh)
    Tinv = [t.astype(BF16) for t in _unit_lower_inverse(Lab, eye, row, col)]
    pdot2 = lambda x, y0, y1: lanes(jnp.dot(
        x, jnp.concatenate([_block_diag(y0), _block_diag(y1)], axis=1), preferred_element_type=F32))
    wu = _each(pdot2, Tinv, at, MakV)
    W = [z[0].astype(BF16) for z in wu]
    U = [z[1].astype(BF16) for z in wu]
    qy = _each(pdot2, Mrb, W, U)
    Q = _each(lambda x, z: x + z[0], rt, qy)
    y_add = _each(lambda z, m: z[1] + m, qy, MrkV)
    ms = _each(lambda w, u, b_: _bdot_tn(jnp.concatenate([w, u], axis=1), b_), W, U, bh)
    m_state = [_fold_diag(z[:LANES]) for z in ms]
    s_add = _each(lambda z, kv: _fold_diag(z[LANES:]) + kv, ms, KV)
    return Q, y_add, m_state, s_add


def _rwkv_kernel(*refs, tb, has_vres, n_cast):
    n_in = 21 if has_vres else 18
    n_out = 1 if has_vres else 2
    ins, refs = refs[:n_in], refs[n_in:]
    cast_in, refs = refs[:n_cast], refs[n_cast:]
    outs, refs = refs[:n_out], refs[n_out:]
    cast_out, refs = refs[:n_cast], refs[n_cast:]
    prev_ref, prevs_ref, S_ref, ybuf_ref = refs
    if has_vres:
        (r_ref, k_ref, v_ref, sm_ref, vfirst_ref,
         mur_ref, muk_ref, muv_ref, mus_ref,
         w0_ref, wup_ref, a0_ref, aup_ref, gup_ref, v0_ref, vup_ref,
         kk_ref, ka_ref, rk_ref, lng_ref, lnb_ref) = ins
        (y_ref,) = outs
    else:
        (r_ref, k_ref, v_ref, sm_ref,
         mur_ref, muk_ref, muv_ref, mus_ref,
         w0_ref, wup_ref, a0_ref, aup_ref, gup_ref,
         kk_ref, ka_ref, rk_ref, lng_ref, lnb_ref) = ins
        y_ref, vfirst_out_ref = outs
    t = pl.program_id(1)

    for src_ref, dst_ref in zip(cast_in, cast_out):
        dst_ref[...] = src_ref[...].astype(BF16)

    @pl.when(t == 0)
    def _():
        prev_ref[...] = jnp.zeros(prev_ref.shape, F32)
        prevs_ref[...] = jnp.zeros(prevs_ref.shape, F32)
        S_ref[...] = jnp.zeros(S_ref.shape, F32)

    def shift_mix(raw, prev_row, mu):
        first = lax.broadcasted_iota(jnp.int32, raw.shape, 0) == 0
        sh = jnp.where(first, prev_row, pltpu.roll(raw, 1, 0))
        return raw + (sh - raw) * mu

    r_raw, k_raw, v_raw, sm_raw = r_ref[...], k_ref[...], v_ref[...], sm_ref[...]
    r = shift_mix(r_raw, prev_ref[0, 7:8, :], mur_ref[...])
    k = shift_mix(k_raw, prev_ref[1, 7:8, :], muk_ref[...])
    v = shift_mix(v_raw, prev_ref[2, 7:8, :], muv_ref[...])
    sm = shift_mix(sm_raw, prevs_ref[7:8, :], mus_ref[...])
    prev_ref[0] = r_raw[tb - 8:tb, :]
    prev_ref[1] = k_raw[tb - 8:tb, :]
    prev_ref[2] = v_raw[tb - 8:tb, :]
    prevs_ref[...] = sm_raw[tb - 8:tb, :]

    dl, al, gl = wup_ref.shape[0], aup_ref.shape[0], gup_ref.shape[0]
    wd = sm[:, 0:dl]
    ad = sm[:, dl:dl + al]
    gd = sm[:, dl + al:dl + al + gl]
    w = -_softplus(-(w0_ref[...] + _bdot(jnp.tanh(wd), wup_ref[...]))) - 0.5
    lw = -jnp.exp(w)
    a = _sigmoid(a0_ref[...] + _bdot(ad, aup_ref[...]))
    g = _bdot(_sigmoid(gd), gup_ref[...])
    if has_vres:
        ml = vup_ref.shape[0]
        vd = sm[:, dl + al + gl:dl + al + gl + ml]
        v = v + (vfirst_ref[...] - v) * _sigmoid(v0_ref[...] + _bdot(vd, vup_ref[...]))
    else:
        vfirst_out_ref[...] = v

    li = lax.broadcasted_iota(jnp.int32, (LANES, LANES), 0) // RWKV_HEAD
    lj = lax.broadcasted_iota(jnp.int32, (LANES, LANES), 1) // RWKV_HEAD
    head_ones = jnp.where(li == lj, 1.0, 0.0).astype(BF16)
    width = y_ref.shape[1]
    n_pairs = width // LANES

    def split3(z):
        hi = z.astype(BF16)
        r1 = z - hi.astype(F32)
        mid = r1.astype(BF16)
        return hi, mid, (r1 - mid.astype(F32)).astype(BF16)

    def head_sum(z):
        parts = split3(z)[:2]
        return jnp.concatenate(
            [sum(jnp.dot(q[:, j * LANES:(j + 1) * LANES], head_ones, preferred_element_type=F32) for q in parts)
             for j in range(n_pairs)], axis=1)

    kk = k * kk_ref[...]
    kk = kk / jnp.maximum(jnp.sqrt(head_sum(kk * kk)), 1e-12)
    k = k * (1.0 + (a - 1.0) * ka_ref[...])

    C = WKV_CHUNK
    n_chunks = tb // C
    ti = lax.broadcasted_iota(jnp.int32, (tb, tb), 0)
    tj = lax.broadcasted_iota(jnp.int32, (tb, tb), 1)
    tri = jnp.where((tj <= ti) & ((ti // C) == (tj // C)), 1.0, 0.0).astype(BF16)
    lc = sum(jnp.dot(tri, q, preferred_element_type=F32) for q in split3(lw))
    lc_end = jnp.concatenate(
        [jnp.broadcast_to(lc[(c + 1) * C - 1:(c + 1) * C, :], (C, width)) for c in range(n_chunks)], axis=0)
    e_neg = jnp.exp(-lc)
    e_end = jnp.exp(lc_end - lc)
    at = (-kk * jnp.exp(lc - lw)).astype(BF16)
    rt = r * jnp.exp(lc)
    bt = (kk * a * e_neg).astype(BF16)
    kt = (k * e_neg).astype(BF16)
    bh = (kk * a * e_end).astype(BF16)
    kh = (k * e_end).astype(BF16)
    vb = v.astype(BF16)
    decay_end = jnp.exp(lc_end)

    tiles = [(c, j) for c in range(n_chunks) for j in range(n_pairs)]
    cut = lambda z: [z[c * C:(c + 1) * C, j * LANES:(j + 1) * LANES] for c, j in tiles]
    Q, y_add, m_state, s_add = _wkv_chunk_operators(*[cut(z) for z in (at, rt, bt, kt, bh, kh, vb)])
    S = [S_ref[j] for j in range(n_pairs)]
    for c in range(n_chunks):
        for j in range(n_pairs):
            i = c * n_pairs + j
            ls = slice(j * LANES, (j + 1) * LANES)
            ybuf_ref[c * C:(c + 1) * C, ls] = _pdot_nt(Q[i], S[j]) + y_add[i]
            S[j] = S[j] * decay_end[c * C:c * C + 1, ls] + _pdot(S[j], m_state[i]) + s_add[i]
    for j in range(n_pairs):
        S_ref[j] = S[j]

    y = ybuf_ref[...]
    inv_n = 1.0 / RWKV_HEAD
    mu = head_sum(y) * inv_n
    yc = y - mu
    var = head_sum(yc * yc) * inv_n
    y = yc * lax.rsqrt(var + RWKV_GN_EPS) * lng_ref[...] + lnb_ref[...]
    y = y + head_sum(r * k * rk_ref[...]) * v
    y_ref[...] = (y * g).astype(y_ref.dtype)


def _cast_plan(shape, n_steps):
    rows, cols = shape
    for ncb in (1, 2, 4, 8, 16):
        rb = n_steps // ncb
        if (n_steps % ncb == 0 and rows % rb == 0 and cols % ncb == 0
                and (rows // rb) % BF16_SUBLANES == 0 and (cols // ncb) % LANES == 0):
            return rb, ncb
    return None


def _rwkv_mixer(proj, G, small_w, small_off, p, v_first, to_cast):
    T = proj.shape[0]
    has_vres = v_first is not None
    tb = _pick(T, (256, 128, 64))
    bw = _pick(G, (RWKV_BLOCK_LANES, 2 * LANES, LANES))
    cb = G // bw
    dl, al, gl = p['decay_up'].shape[0], p['iclr_up'].shape[0], p['gate_up'].shape[0]
    ml = p['vres_up'].shape[0] if has_vres else 0
    mu = p['shift_mu']
    mu_r, mu_k, mu_v = (mu[q * G:(q + 1) * G].reshape(1, G) for q in range(3))
    mu_s = jnp.pad(mu[3 * G:], (0, small_w - (dl + al + gl + ml))).reshape(1, small_w)

    def colblk(first):
        return pl.BlockSpec((tb, bw), lambda q, t, f=first: (t, f + q))

    def vec(arr):
        return arr.reshape(1, G), pl.BlockSpec((1, bw), lambda q, t: (0, q))

    def up(arr):
        return arr.astype(BF16), pl.BlockSpec((arr.shape[0], bw), lambda q, t: (0, q))

    small_spec = pl.BlockSpec((tb, small_w), lambda q, t: (t, small_off // small_w))
    pair_spec = pl.BlockSpec((tb, bw), lambda q, t: (t, q))
    mu_spec = pl.BlockSpec((1, bw), lambda q, t: (0, q))
    args = [proj, proj, proj, proj]
    specs = [colblk(3 * cb), colblk(4 * cb), colblk(5 * cb), small_spec]
    if has_vres:
        args.append(v_first)
        specs.append(pair_spec)
    args += [mu_r, mu_k, mu_v, mu_s]
    specs += [mu_spec, mu_spec, mu_spec, pl.BlockSpec((1, small_w), lambda q, t: (0, 0))]
    names = ['decay_w0', 'decay_up', 'iclr_a0', 'iclr_up', 'gate_up']
    if has_vres:
        names += ['vres_v0', 'vres_up']
    names += ['k_k', 'k_a', 'r_k', 'lnx_g', 'lnx_b']
    for nm in names:
        arr, spec = up(p[nm]) if nm.endswith('_up') else vec(p[nm])
        args.append(arr)
        specs.append(spec)
    out_shape = [jax.ShapeDtypeStruct((T, G), BF16)]
    out_specs = [pair_spec]
    if not has_vres:
        out_shape.append(jax.ShapeDtypeStruct((T, G), F32))
        out_specs.append(pair_spec)
    n_t = T // tb
    n_steps = (G // bw) * n_t
    cast_names, cast_out = [], {}
    for nm, arr in to_cast.items():
        plan = _cast_plan(arr.shape, n_steps)
        if plan is None:
            cast_out[nm] = arr.astype(BF16)
            continue
        rb, ncb = plan
        spec = pl.BlockSpec((arr.shape[0] // rb, arr.shape[1] // ncb),
                            lambda q, t, ncb=ncb: ((q * n_t + t) // ncb, (q * n_t + t) % ncb))
        cast_names.append(nm)
        args.append(arr)
        specs.append(spec)
        out_shape.append(jax.ShapeDtypeStruct(arr.shape, BF16))
        out_specs.append(spec)
    n_main_out = 1 if has_vres else 2
    outs = pl.pallas_call(
        functools.partial(_rwkv_kernel, tb=tb, has_vres=has_vres, n_cast=len(cast_names)),
        grid=(G // bw, T // tb),
        in_specs=specs,
        out_specs=out_specs,
        out_shape=out_shape,
        scratch_shapes=[pltpu.VMEM((3, 8, bw), F32), pltpu.VMEM((8, small_w), F32),
                        pltpu.VMEM((bw // LANES, RWKV_HEAD, LANES), F32),
                        pltpu.VMEM((tb, bw), F32)],
        compiler_params=_cparams(("arbitrary", "arbitrary")),
        name="rwkv7_mixer",
    )(*args)
    cast_out.update(zip(cast_names, outs[n_main_out:]))
    return outs[0], (v_first if has_vres else outs[1]), cast_out


def _outproj_kernel(y0_ref, y1_ref, y2_ref, y3_ref, w_ref, x_ref, g_ref, b_ref, o_ref, ycat_ref, *, tn):
    G = y0_ref.shape[1]
    for j, y_ref in enumerate((y0_ref, y1_ref, y2_ref, y3_ref)):
        ycat_ref[:, j * G:(j + 1) * G] = y_ref[...]
    ycat = ycat_ref[...]
    for n in range(0, o_ref.shape[1], tn):
        o_ref[:, n:n + tn] = (DEEPNORM_ALPHA * x_ref[:, n:n + tn]
                              + jnp.dot(ycat, w_ref[:, n:n + tn], preferred_element_type=F32))
    o_ref[...] = _layer_norm(o_ref[...], g_ref[...], b_ref[...], LN_EPS)


def _outproj_ln(ys, w, x, g, b):
    T, D = x.shape
    G = ys[0].shape[1]
    tm = _pick(T, (256, 128))
    yspec = pl.BlockSpec((tm, G), lambda i: (i, 0))
    row = pl.BlockSpec((tm, D), lambda i: (i, 0))
    vec = pl.BlockSpec((1, D), lambda i: (0, 0))
    wspec = pl.BlockSpec(w.shape, lambda i: (0, 0), pipeline_mode=pl.Buffered(1))
    return pl.pallas_call(
        functools.partial(_outproj_kernel, tn=_pick(D, (512, 256, 128))),
        grid=(T // tm,),
        in_specs=[yspec] * 4 + [wspec, row, vec, vec],
        out_specs=row,
        out_shape=jax.ShapeDtypeStruct((T, D), F32),
        scratch_shapes=[pltpu.VMEM((tm, len(ys) * G), BF16)],
        compiler_params=_cparams(("arbitrary",)),
        name="outproj_ln",
    )(*ys, w, x, g.reshape(1, D), b.reshape(1, D))


def _ffn_kernel(x_hbm, wg_ref, wu_ref, wd_ref, g_ref, b_ref, o_hbm, acc_ref, xb_ref, in_sem, out_sem, *, tn):
    i, ni = pl.program_id(0), pl.num_programs(0)
    f, nf = pl.program_id(1), pl.num_programs(1)
    tm = xb_ref.shape[0]
    tr = _pick(tm, (256, 128))
    slot = i % 2

    def rows(c):
        return pl.ds(pl.multiple_of(c * tr, tr), tr)

    def x_copy(blk, s):
        return pltpu.make_async_copy(x_hbm.at[pl.ds(blk * tm, tm), :], acc_ref.at[s], in_sem.at[0])

    def out_copy(blk, s):
        return pltpu.make_async_copy(acc_ref.at[s], o_hbm.at[pl.ds(blk * tm, tm), :], out_sem.at[s])

    @pl.when(f == 0)
    def _():
        @pl.when(i == 0)
        def _():
            x_copy(0, 0).start()

        x_copy(i, slot).wait()

        def body(c, carry):
            xr = acc_ref[slot, rows(c), :]
            xb_ref[rows(c), :] = xr.astype(BF16)
            acc_ref[slot, rows(c), :] = DEEPNORM_ALPHA * xr
            return carry
        lax.fori_loop(0, tm // tr, body, 0)

    xb = xb_ref[...]
    gate = jnp.dot(xb, wg_ref[...], preferred_element_type=F32)
    upv = jnp.dot(xb, wu_ref[...], preferred_element_type=F32)
    hid = (gate * _sigmoid(gate) * upv).astype(BF16)
    for n in range(0, acc_ref.shape[2], tn):
        acc_ref[slot, :, n:n + tn] += jnp.dot(hid, wd_ref[:, n:n + tn], preferred_element_type=F32)

    @pl.when(f == jnp.maximum(nf - 2, 0))
    def _():
        @pl.when(i >= 1)
        def _():
            out_copy(i - 1, 1 - slot).wait()

        @pl.when(i + 1 < ni)
        def _():
            x_copy(i + 1, 1 - slot).start()

    @pl.when(f == nf - 1)
    def _():
        def body(c, carry):
            acc_ref[slot, rows(c), :] = _layer_norm(acc_ref[slot, rows(c), :], g_ref[...], b_ref[...], LN_EPS)
            return carry
        lax.fori_loop(0, tm // tr, body, 0)
        out_copy(i, slot).start()

        @pl.when(i == ni - 1)
        def _():
            out_copy(i, slot).wait()


def _ffn_ln(x, wg, wu, wd, g, b):
    T, D = x.shape
    F = wg.shape[1]
    tm = _pick(T, (1024, 512, 256, 128))
    tf = _pick(F, (256, 128))
    vec = pl.BlockSpec((1, D), lambda i, f: (0, 0))
    return pl.pallas_call(
        functools.partial(_ffn_kernel, tn=_pick(D, (512, 256, 128))),
        grid=(T // tm, F // tf),
        in_specs=[pl.BlockSpec(memory_space=pl.ANY),
                  pl.BlockSpec((D, tf), lambda i, f: (0, f)), pl.BlockSpec((D, tf), lambda i, f: (0, f)),
                  pl.BlockSpec((tf, D), lambda i, f: (f, 0)), vec, vec],
        out_specs=pl.BlockSpec(memory_space=pl.ANY),
        out_shape=jax.ShapeDtypeStruct((T, D), F32),
        scratch_shapes=[pltpu.VMEM((2, tm, D), F32), pltpu.VMEM((tm, D), BF16),
                        pltpu.SemaphoreType.DMA((1,)), pltpu.SemaphoreType.DMA((2,))],
        compiler_params=_cparams(("arbitrary", "arbitrary")),
        name="ffn_ln",
    )(x, wg, wu, wd, g.reshape(1, D), b.reshape(1, D))


def _hybrid_layer(x, v_first, p, w_in_t, next_w_in):
    T, D = x.shape
    G = D // 4
    n_small = w_in_t.shape[0] - 9 * G
    small_w = _round_up(n_small, 2 * LANES)
    later = {nm: p[nm] for nm in ('w_out', 'ffn_gate', 'ffn_up', 'ffn_down')}
    proj, wb = _inproj(x, w_in_t, 6 * G, n_small, 9 * G + small_w, later)
    y_conv, y_sgu, y_pool = _local_mixers(proj, G, p)
    nxt = {} if next_w_in is None else {'w_in_next_t': next_w_in.T}
    y_rwkv, v_first, wn = _rwkv_mixer(proj, G, small_w, 9 * G, p, v_first, nxt)
    x = _outproj_ln((y_conv, y_rwkv, y_sgu, y_pool), wb['w_out'], x, p['ln_mix_g'], p['ln_mix_b'])
    x = _ffn_ln(x, wb['ffn_gate'], wb['ffn_up'], wb['ffn_down'], p['ln_ffn_g'], p['ln_ffn_b'])
    return x, v_first, wn.get('w_in_next_t')


_NAMES_0 = ('w_in', 'conv_w', 'shift_mu', 'decay_w0', 'decay_up', 'iclr_a0', 'iclr_up', 'gate_up',
            'k_k', 'k_a', 'r_k', 'lnx_g', 'lnx_b', 'sgu_ln_g', 'sgu_ln_b', 'sgu_w', 'sgu_b',
            'pool_w', 'pool_scale', 'w_out', 'ln_mix_g', 'ln_mix_b', 'ffn_gate', 'ffn_up', 'ffn_down',
            'ln_ffn_g', 'ln_ffn_b')
_NAMES_1 = _NAMES_0[:7] + ('vres_v0', 'vres_up') + _NAMES_0[7:]


def kernel(x, w_in_0, conv_w_0, shift_mu_0, decay_w0_0, decay_up_0, iclr_a0_0, iclr_up_0, gate_up_0, k_k_0, k_a_0, r_k_0, lnx_g_0, lnx_b_0, sgu_ln_g_0, sgu_ln_b_0, sgu_w_0, sgu_b_0, pool_w_0, pool_scale_0, w_out_0, ln_mix_g_0, ln_mix_b_0, ffn_gate_0, ffn_up_0, ffn_down_0, ln_ffn_g_0, ln_ffn_b_0, w_in_1, conv_w_1, shift_mu_1, decay_w0_1, decay_up_1, iclr_a0_1, iclr_up_1, vres_v0_1, vres_up_1, gate_up_1, k_k_1, k_a_1, r_k_1, lnx_g_1, lnx_b_1, sgu_ln_g_1, sgu_ln_b_1, sgu_w_1, sgu_b_1, pool_w_1, pool_scale_1, w_out_1, ln_mix_g_1, ln_mix_b_1, ffn_gate_1, ffn_up_1, ffn_down_1, ln_ffn_g_1, ln_ffn_b_1):
    p0 = dict(zip(_NAMES_0, (w_in_0, conv_w_0, shift_mu_0, decay_w0_0, decay_up_0, iclr_a0_0, iclr_up_0, gate_up_0, k_k_0, k_a_0, r_k_0, lnx_g_0, lnx_b_0, sgu_ln_g_0, sgu_ln_b_0, sgu_w_0, sgu_b_0, pool_w_0, pool_scale_0, w_out_0, ln_mix_g_0, ln_mix_b_0, ffn_gate_0, ffn_up_0, ffn_down_0, ln_ffn_g_0, ln_ffn_b_0)))
    p1 = dict(zip(_NAMES_1, (w_in_1, conv_w_1, shift_mu_1, decay_w0_1, decay_up_1, iclr_a0_1, iclr_up_1, vres_v0_1, vres_up_1, gate_up_1, k_k_1, k_a_1, r_k_1, lnx_g_1, lnx_b_1, sgu_ln_g_1, sgu_ln_b_1, sgu_w_1, sgu_b_1, pool_w_1, pool_scale_1, w_out_1, ln_mix_g_1, ln_mix_b_1, ffn_gate_1, ffn_up_1, ffn_down_1, ln_ffn_g_1, ln_ffn_b_1)))
    B, T, D = x.shape
    assert B == 1
    h = x.reshape(T, D)
    h, v_first, w_in_t = _hybrid_layer(h, None, p0, w_in_0.T, w_in_1)
    h, _, _ = _hybrid_layer(h, v_first, p1, w_in_t, None)
    return h.reshape(B, T, D)
```
